```python
import jax, jax.numpy as jnp
from jax import lax
import numpy as np

D_MODEL = 1024
BATCH = 32
SEQ = 2048
DEPTH = 2
DEC_BATCH = 8
DEC_SEQ = 2048
PAST_LEN = 128

GRID_W = 64
N_HEADS = 8
N_KV_HEADS = 2
HEAD_DIM = 64
Q_BLOCK = 128
ROPE_THETA = 10000.0
ATTN_WIDTH = N_HEADS * HEAD_DIM
KV_WIDTH = N_KV_HEADS * HEAD_DIM
CONV_WIDTH = 512
CONV_KERNEL = 31
N_BRANCHES = 2
IN_COLS = ATTN_WIDTH + 2 * KV_WIDTH + 2 * CONV_WIDTH + N_BRANCHES * D_MODEL
MEM_LEN = 256
N_XHEADS = 4
XHEAD_DIM = D_MODEL // N_XHEADS
N_GROUPS = 4
EXPERTS_PER_GROUP = 8
N_EXPERTS = N_GROUPS * EXPERTS_PER_GROUP
TOP_K = 2
D_FF_EXPERT = 512
EXPERT_BLOCK = 256
EPS = 1e-6

kernel_name = "hybrid_gqa_conformer_hiermoe_encoder"


def rms_norm(x, g):
    xf = x.astype(jnp.float32)
    y = xf * lax.rsqrt(jnp.mean(xf * xf, axis=-1, keepdims=True) + EPS)
    return (y * g.astype(jnp.float32)).astype(x.dtype)


def layer_norm(x, g, b):
    xf = x.astype(jnp.float32)
    mu = jnp.mean(xf, axis=-1, keepdims=True)
    xc = xf - mu
    y = xc * lax.rsqrt(jnp.mean(xc * xc, axis=-1, keepdims=True) + EPS)
    return (y * g.astype(jnp.float32) + b.astype(jnp.float32)).astype(x.dtype)


def axial_rope_angles(seq_len):
    rows = seq_len // GRID_W
    row_idx = jnp.repeat(jnp.arange(rows, dtype=jnp.float32), GRID_W)
    col_idx = jnp.tile(jnp.arange(GRID_W, dtype=jnp.float32), rows)
    n_freq = HEAD_DIM // 4
    inv_freq = ROPE_THETA ** (-jnp.arange(n_freq, dtype=jnp.float32) / n_freq)
    return row_idx[:, None] * inv_freq, col_idx[:, None] * inv_freq


def rope_rotate(x, ang):
    f = ang.shape[-1]
    cos = jnp.cos(ang)[:, None, :]
    sin = jnp.sin(ang)[:, None, :]
    x1 = x[..., :f].astype(jnp.float32)
    x2 = x[..., f:].astype(jnp.float32)
    return jnp.concatenate([x1 * cos - x2 * sin, x2 * cos + x1 * sin], axis=-1).astype(x.dtype)


def apply_axial_rope(x, ang_row, ang_col):
    half = HEAD_DIM // 2
    return jnp.concatenate([rope_rotate(x[..., :half], ang_row),
                            rope_rotate(x[..., half:], ang_col)], axis=-1)


def blocked_gqa(q, k, v):
    b, s = q.shape[0], q.shape[1]
    g = N_HEADS // N_KV_HEADS
    nqb = s // Q_BLOCK
    qb = q.reshape(b, nqb, Q_BLOCK, N_KV_HEADS, g, HEAD_DIM).transpose(1, 0, 2, 3, 4, 5)
    scale = HEAD_DIM ** -0.5

    def one_block(qblk):
        sc = jnp.einsum('bqkgd,bskd->bkgqs', qblk, k).astype(jnp.float32) * scale
        p = jax.nn.softmax(sc, axis=-1).astype(v.dtype)
        return jnp.einsum('bkgqs,bskd->bqkgd', p, v)

    o = lax.map(one_block, qb)
    return o.transpose(1, 0, 2, 3, 4, 5).reshape(b, s, ATTN_WIDTH)


def mixer_block(h, ang_row, ang_col, w_in, g_q, g_k, b_gate, w_attn_up,
                conv_w, conv_b, ln_conv_g, ln_conv_b, w_conv_out, w_out):
    b, s, d = h.shape
    proj = h @ w_in
    splits = np.cumsum([ATTN_WIDTH, KV_WIDTH, KV_WIDTH, CONV_WIDTH, CONV_WIDTH]).tolist()
    q, k, v, c_a, c_b, gate_logits = jnp.split(proj, splits, axis=-1)
    q = apply_axial_rope(rms_norm(q.reshape(b, s, N_HEADS, HEAD_DIM), g_q), ang_row, ang_col)
    k = apply_axial_rope(rms_norm(k.reshape(b, s, N_KV_HEADS, HEAD_DIM), g_k), ang_row, ang_col)
    v = v.reshape(b, s, N_KV_HEADS, HEAD_DIM)
    attn = blocked_gqa(q, k, v) @ w_attn_up
    u = c_a * jax.nn.sigmoid(c_b)
    u = lax.conv_general_dilated(
        u, conv_w[:, None, :], window_strides=(1,),
        padding=[(CONV_KERNEL // 2, CONV_KERNEL // 2)],
        dimension_numbers=('NWC', 'WIO', 'NWC'),
        feature_group_count=CONV_WIDTH) + conv_b
    u = jax.nn.silu(layer_norm(u, ln_conv_g, ln_conv_b))
    conv = u @ w_conv_out
    gates = jax.nn.sigmoid(gate_logits + b_gate).reshape(b, s, N_BRANCHES, d)
    merged = gates[:, :, 0, :] * attn + gates[:, :, 1, :] * conv
    return merged @ w_out


def memory_cross_attention(h, mem, g_mem, w_xq, w_xkv, w_xo):
    b, s, d = h.shape
    m = mem.shape[1]
    q = (h @ w_xq).reshape(b, s, N_XHEADS, XHEAD_DIM)
    k, v = jnp.split(rms_norm(mem, g_mem) @ w_xkv, 2, axis=-1)
    k = k.reshape(b, m, N_XHEADS, XHEAD_DIM)
    v = v.reshape(b, m, N_XHEADS, XHEAD_DIM)
    sc = jnp.einsum('bqhd,bmhd->bhqm', q, k).astype(jnp.float32) * (XHEAD_DIM ** -0.5)
    p = jax.nn.softmax(sc, axis=-1).astype(v.dtype)
    o = jnp.einsum('bhqm,bmhd->bqhd', p, v).reshape(b, s, d)
    return o @ w_xo


def hierarchical_moe(h, w_group, b_group, w_router, b_router, w_e_gate, w_e_up, w_e_down):
    b, s, d = h.shape
    x = h.reshape(-1, d)
    n = x.shape[0]
    p_group = jax.nn.softmax((x @ w_group).astype(jnp.float32) + b_group.astype(jnp.float32), axis=-1)
    pg, grp = lax.top_k(p_group, 1)
    logits_e = ((x @ w_router).astype(jnp.float32) + b_router.astype(jnp.float32)).reshape(
        n, N_GROUPS, EXPERTS_PER_GROUP)
    logits_in = jnp.take_along_axis(logits_e, grp[:, :, None], axis=1)[:, 0]
    top_val, top_idx = lax.top_k(logits_in, TOP_K)
    gate = pg * jax.nn.softmax(top_val, axis=-1)
    eid = (grp * EXPERTS_PER_GROUP + top_idx).astype(jnp.int32)
    m = n * TOP_K
    flat_e = eid.reshape(m)
    order = jnp.argsort(flat_e).astype(jnp.int32)
    sorted_e = flat_e[order]
    counts = jnp.bincount(flat_e, length=N_EXPERTS).astype(jnp.int32)
    starts = jnp.cumsum(counts) - counts
    padded = (counts + EXPERT_BLOCK - 1) // EXPERT_BLOCK * EXPERT_BLOCK
    pad_end = jnp.cumsum(padded)
    pad_start = pad_end - padded
    dest_sorted = pad_start[sorted_e] + jnp.arange(m, dtype=jnp.int32) - starts[sorted_e]
    n_blocks = (m + N_EXPERTS * (EXPERT_BLOCK - 1) + EXPERT_BLOCK - 1) // EXPERT_BLOCK
    cap = n_blocks * EXPERT_BLOCK
    row_token = jnp.full((cap,), n, jnp.int32).at[dest_sorted].set(order // TOP_K)
    x_pad = jnp.concatenate([x, jnp.zeros((1, d), x.dtype)], axis=0)
    x_buf = x_pad[row_token].reshape(n_blocks, EXPERT_BLOCK, d)
    block_e = jnp.minimum(
        jnp.searchsorted(pad_end, jnp.arange(n_blocks, dtype=jnp.int32) * EXPERT_BLOCK, side='right'),
        N_EXPERTS - 1).astype(jnp.int32)

    def expert_block(args):
        xb, e = args
        hid = jax.nn.silu(xb @ w_e_gate[e]) * (xb @ w_e_up[e])
        return hid @ w_e_down[e]

    y_buf = lax.map(expert_block, (x_buf, block_e)).reshape(cap, d)
    dest = jnp.zeros((m,), jnp.int32).at[order].set(dest_sorted).reshape(n, TOP_K)
    y = jnp.einsum('nk,nkd->nd', gate.astype(x.dtype), y_buf[dest])
    return y.reshape(b, s, d)


def setup_inputs(seed: int = 0) -> dict:
    key = jax.random.key(seed)
    ks = list(jax.random.split(key, 32))
    nrm = jax.random.normal

    def dense(k, shape, fan_in):
        return nrm(k, shape, jnp.float32) * (fan_in ** -0.5)

    def gain(k, shape):
        return 1.0 + 0.05 * nrm(k, shape, jnp.float32)

    def bias(k, shape, scale=0.02):
        return scale * nrm(k, shape, jnp.float32)

    L, D = DEPTH, D_MODEL
    return {
        "x_prompt": nrm(ks[0], (BATCH, SEQ, D), jnp.float32),
        "x_sample": nrm(ks[1], (DEC_BATCH, DEC_SEQ, D), jnp.float32),
        "mem_prompt": nrm(ks[2], (BATCH, MEM_LEN, D), jnp.float32),
        "mem_sample": nrm(ks[3], (DEC_BATCH, MEM_LEN, D), jnp.float32),
        "g_mix": gain(ks[4], (L, D)),
        "w_in": dense(ks[5], (L, D, IN_COLS), D),
        "g_q": gain(ks[6], (L, HEAD_DIM)),
        "g_k": gain(ks[7], (L, HEAD_DIM)),
        "b_gate": bias(ks[8], (L, N_BRANCHES * D)),
        "w_attn_up": dense(ks[9], (L, ATTN_WIDTH, D), ATTN_WIDTH),
        "conv_w": dense(ks[10], (L, CONV_KERNEL, CONV_WIDTH), CONV_KERNEL),
        "conv_b": bias(ks[11], (L, CONV_WIDTH)),
        "ln_conv_g": gain(ks[12], (L, CONV_WIDTH)),
        "ln_conv_b": bias(ks[13], (L, CONV_WIDTH)),
        "w_conv_out": dense(ks[14], (L, CONV_WIDTH, D), CONV_WIDTH),
        "w_out": dense(ks[15], (L, D, D), D),
        "g_cross": gain(ks[16], (L, D)),
        "g_mem": gain(ks[17], (L, D)),
        "w_xq": dense(ks[18], (L, D, D), D),
        "w_xkv": dense(ks[19], (L, D, 2 * D), D),
        "w_xo": dense(ks[20], (L, D, D), D),
        "g_ffn": gain(ks[21], (L, D)),
        "w_group": dense(ks[22], (L, D, N_GROUPS), D),
        "b_group": bias(ks[23], (L, N_GROUPS), 0.01),
        "w_router": dense(ks[24], (L, D, N_EXPERTS), D),
        "b_router": bias(ks[25], (L, N_EXPERTS), 0.01),
        "w_e_gate": dense(ks[26], (L, N_EXPERTS, D, D_FF_EXPERT), D),
        "w_e_up": dense(ks[27], (L, N_EXPERTS, D, D_FF_EXPERT), D),
        "w_e_down": dense(ks[28], (L, N_EXPERTS, D_FF_EXPERT, D), D_FF_EXPERT),
        "g_final": gain(ks[29], (D,)),
    }


def reference(x_prompt, x_sample, mem_prompt, mem_sample, g_mix, w_in, g_q, g_k, b_gate,
              w_attn_up, conv_w, conv_b, ln_conv_g, ln_conv_b, w_conv_out, w_out,
              g_cross, g_mem, w_xq, w_xkv, w_xo, g_ffn, w_group, b_group, w_router,
              b_router, w_e_gate, w_e_up, w_e_down, g_final):

    def run(x, mem):
        ang_row, ang_col = axial_rope_angles(x.shape[1])
        for l in range(DEPTH):
            h = rms_norm(x, g_mix[l])
            x = x + mixer_block(h, ang_row, ang_col, w_in[l], g_q[l], g_k[l], b_gate[l],
                                w_attn_up[l], conv_w[l], conv_b[l], ln_conv_g[l],
                                ln_conv_b[l], w_conv_out[l], w_out[l])
            x = x + memory_cross_attention(rms_norm(x, g_cross[l]), mem, g_mem[l],
                                           w_xq[l], w_xkv[l], w_xo[l])
            x = x + hierarchical_moe(rms_norm(x, g_ffn[l]), w_group[l], b_group[l],
                                     w_router[l], b_router[l], w_e_gate[l], w_e_up[l],
                                     w_e_down[l])
        return rms_norm(x, g_final)

    y_prompt = run(x_prompt, mem_prompt)
    y_sample = run(x_sample, mem_sample)
    return (y_prompt, y_sample)
```

```python
import functools

import numpy as np
import jax
import jax.numpy as jnp
from jax import lax
from jax.experimental import pallas as pl
from jax.experimental.pallas import tpu as pltpu

F32 = jnp.float32
BF16 = jnp.bfloat16
U32 = jnp.uint32
I32 = jnp.int32

EPS = 1e-6
D_MODEL = 1024
GRID_W = 64
N_HEADS = 8
N_KV_HEADS = 2
HEAD_DIM = 64
ROPE_THETA = 10000.0
ATTN_WIDTH = N_HEADS * HEAD_DIM
KV_WIDTH = N_KV_HEADS * HEAD_DIM
CONV_WIDTH = 512
CONV_KERNEL = 31
N_XHEADS = 4
XHEAD_DIM = D_MODEL // N_XHEADS
N_GROUPS = 4
EXPERTS_PER_GROUP = 8
N_EXPERTS = N_GROUPS * EXPERTS_PER_GROUP
D_FF_EXPERT = 512

V7X_VMEM_BYTES = 64 * 1024 * 1024
LANES = 128
SUBLANES = 8
EXPERT_ROWS = 256
ROUTER_ROWS = 40
PACKED = D_MODEL // 2
NEG_BIG = float(np.finfo(np.float32).min)
NT_DIMS = (((1,), (1,)), ((), ()))


def _cparams(n_axes, vmem_mib):
    return pltpu.CompilerParams(
        dimension_semantics=("arbitrary",) * n_axes,
        vmem_limit_bytes=min(vmem_mib * 1024 * 1024, V7X_VMEM_BYTES - 8 * 1024 * 1024))


def _dot(a, b):
    return jnp.dot(a, b, preferred_element_type=F32)


def _sigmoid(x):
    return 1.0 / (1.0 + jnp.exp(-x))


def _rms(x, g):
    return x * lax.rsqrt(jnp.mean(x * x, axis=-1, keepdims=True) + EPS) * g


def _pack_bf16_pairs(y):
    w = y.shape[1] // 2
    yb = y.astype(BF16).astype(F32)
    hi = lax.bitcast_convert_type(yb[:, :w], U32)
    lo = lax.bitcast_convert_type(yb[:, w:], U32)
    return (hi & jnp.uint32(0xFFFF0000)) | (lo >> 16)


def _unpack_bf16_pairs(p):
    hi = lax.bitcast_convert_type(p & jnp.uint32(0xFFFF0000), F32)
    lo = lax.bitcast_convert_type(p << 16, F32)
    return jnp.concatenate([hi, lo], axis=1)


def _head_norm_rope(raw, gain, seg, cos, sin, scale):
    width = raw.shape[1]
    sq = raw * raw
    sq_hi = sq.astype(BF16)
    sq_lo = (sq - sq_hi.astype(F32)).astype(BF16)
    ss = _dot(sq_hi, seg) + _dot(sq_lo, seg)
    qn = raw * lax.rsqrt(ss * (1.0 / HEAD_DIM) + EPS) * gain
    reps = width // LANES
    cos_w = jnp.concatenate([cos] * reps, axis=1) if reps > 1 else cos
    sin_w = jnp.concatenate([sin] * reps, axis=1) if reps > 1 else sin
    lane = lax.broadcasted_iota(I32, qn.shape, 1)
    first_half = (lane & 16) == 0
    partner = jnp.where(first_half, pltpu.roll(qn, width - 16, 1), pltpu.roll(qn, 16, 1))
    return (qn * cos_w + partner * sin_w) * scale


def _inproj_kernel(x_ref, gmix_ref, w_ref, gq_ref, gk_ref, cos_ref, sin_ref, bg_ref, seg_ref,
                   q_ref, k_ref, v_ref, u_ref, gate_ref):
    h = _rms(x_ref[...], gmix_ref[...]).astype(BF16)
    cos = cos_ref[...]
    sin = sin_ref[...]
    c0 = 0
    q_raw = _dot(h, w_ref[:, c0:c0 + ATTN_WIDTH])
    q_ref[...] = _head_norm_rope(q_raw, gq_ref[...], seg_ref[...], cos, sin, HEAD_DIM ** -0.5).astype(BF16)
    c0 += ATTN_WIDTH
    k_raw = _dot(h, w_ref[:, c0:c0 + KV_WIDTH])
    k_ref[...] = _head_norm_rope(k_raw, gk_ref[...], seg_ref[:KV_WIDTH, :KV_WIDTH], cos, sin, 1.0).astype(BF16)
    c0 += KV_WIDTH
    v_ref[...] = _dot(h, w_ref[:, c0:c0 + KV_WIDTH]).astype(BF16)
    c0 += KV_WIDTH
    c_a = _dot(h, w_ref[:, c0:c0 + CONV_WIDTH])
    c0 += CONV_WIDTH
    c_b = _dot(h, w_ref[:, c0:c0 + CONV_WIDTH])
    c0 += CONV_WIDTH
    u_ref[...] = (c_a * _sigmoid(c_b)).astype(BF16)
    chunk = 512
    for j in range(2 * D_MODEL // chunk):
        lg = _dot(h, w_ref[:, c0 + j * chunk:c0 + (j + 1) * chunk]) + bg_ref[:, j * chunk:(j + 1) * chunk]
        gate_ref[:, j * chunk:(j + 1) * chunk] = _sigmoid(lg).astype(BF16)


def _inproj(x, gmix, w_in, gq_t, gk_t, cos_t, sin_t, b_gate, seg, seq, tm):
    n = x.shape[0]
    in_cols = w_in.shape[1]
    n_seq_tiles = seq // tm
    row = lambda i: (i, 0)
    const = lambda i: (0, 0)
    pos = lambda i: (i % n_seq_tiles, 0)
    return pl.pallas_call(
        _inproj_kernel,
        grid=(n // tm,),
        in_specs=[
            pl.BlockSpec((tm, D_MODEL), row),
            pl.BlockSpec((1, D_MODEL), const),
            pl.BlockSpec((D_MODEL, in_cols), const),
            pl.BlockSpec((1, ATTN_WIDTH), const),
            pl.BlockSpec((1, KV_WIDTH), const),
            pl.BlockSpec((tm, LANES), pos),
            pl.BlockSpec((tm, LANES), pos),
            pl.BlockSpec((1, 2 * D_MODEL), const),
            pl.BlockSpec((ATTN_WIDTH, ATTN_WIDTH), const),
        ],
        out_specs=[
            pl.BlockSpec((tm, ATTN_WIDTH), row),
            pl.BlockSpec((tm, KV_WIDTH), row),
            pl.BlockSpec((tm, KV_WIDTH), row),
            pl.BlockSpec((tm, CONV_WIDTH), row),
            pl.BlockSpec((tm, 2 * D_MODEL), row),
        ],
        out_shape=[
            jax.ShapeDtypeStruct((n, ATTN_WIDTH), BF16),
            jax.ShapeDtypeStruct((n, KV_WIDTH), BF16),
            jax.ShapeDtypeStruct((n, KV_WIDTH), BF16),
            jax.ShapeDtypeStruct((n, CONV_WIDTH), BF16),
            jax.ShapeDtypeStruct((n, 2 * D_MODEL), BF16),
        ],
        compiler_params=_cparams(1, 48),
        name="inproj",
    )(x, gmix, w_in, gq_t, gk_t, cos_t, sin_t, b_gate, seg)


def _attn_kernel(q_ref, k_ref, v_ref, o_ref):
    tq = q_ref.shape[0]
    group = N_HEADS // N_KV_HEADS
    outs = []
    for g in range(N_KV_HEADS):
        kg = k_ref[:, g * HEAD_DIM:(g + 1) * HEAD_DIM]
        vg = v_ref[:, g * HEAD_DIM:(g + 1) * HEAD_DIM]
        qs = jnp.concatenate(
            [q_ref[:, (group * g + h) * HEAD_DIM:(group * g + h + 1) * HEAD_DIM] for h in range(group)], axis=0)
        s = lax.dot_general(qs, kg, NT_DIMS, preferred_element_type=F32)
        m = jnp.max(s, axis=-1, keepdims=True)
        p = jnp.exp(s - m)
        l = jnp.sum(p, axis=-1, keepdims=True)
        o = _dot(p.astype(BF16), vg) / l
        outs.extend(o[h * tq:(h + 1) * tq] for h in range(group))
    o_ref[...] = jnp.concatenate(outs, axis=1).astype(BF16)


def _attention(q, k, v, batch, seq, tq):
    n = q.shape[0]
    nq = seq // tq
    return pl.pallas_call(
        _attn_kernel,
        grid=(batch, nq),
        in_specs=[
            pl.BlockSpec((tq, ATTN_WIDTH), lambda b, i: (b * nq + i, 0)),
            pl.BlockSpec((seq, KV_WIDTH), lambda b, i: (b, 0)),
            pl.BlockSpec((seq, KV_WIDTH), lambda b, i: (b, 0)),
        ],
        out_specs=pl.BlockSpec((tq, ATTN_WIDTH), lambda b, i: (b * nq + i, 0)),
        out_shape=jax.ShapeDtypeStruct((n, ATTN_WIDTH), BF16),
        compiler_params=_cparams(2, 48),
        name="gqa_attention",
    )(q, k, v)


CONV_HALO = 16
CONV_CHUNK = 64


def _conv_kernel(u_ref, w_ref, b_ref, g_ref, beta_ref, o_ref, pad_ref):
    seq = u_ref.shape[0]
    zeros = jnp.zeros((CONV_HALO, CONV_WIDTH), F32)
    pad_ref[0:CONV_HALO, :] = zeros
    pad_ref[CONV_HALO + seq:CONV_HALO + seq + CONV_HALO, :] = zeros
    pad_ref[CONV_HALO:CONV_HALO + seq, :] = u_ref[...].astype(F32)
    shift = CONV_HALO - CONV_KERNEL // 2

    def chunk(c, carry):
        t0 = pl.multiple_of(c * CONV_CHUNK, CONV_CHUNK)
        win = pad_ref[pl.ds(t0, CONV_CHUNK + 2 * CONV_HALO), :]
        acc = jnp.zeros((CONV_CHUNK, CONV_WIDTH), F32)
        span = CONV_CHUNK + (CONV_KERNEL // SUBLANES) * SUBLANES
        for res in range(SUBLANES):
            shifted = win[res:res + span]
            for tap in range(CONV_KERNEL):
                if (tap + shift) % SUBLANES == res:
                    off = (tap + shift) - res
                    acc = acc + shifted[off:off + CONV_CHUNK] * w_ref[tap:tap + 1, :]
        acc = acc + b_ref[...]
        mu = jnp.mean(acc, axis=-1, keepdims=True)
        xc = acc - mu
        y = xc * lax.rsqrt(jnp.mean(xc * xc, axis=-1, keepdims=True) + EPS) * g_ref[...] + beta_ref[...]
        o_ref[pl.ds(t0, CONV_CHUNK), :] = (y * _sigmoid(y)).astype(BF16)
        return carry

    lax.fori_loop(0, seq // CONV_CHUNK, chunk, 0)


def _conv_branch(u, conv_w, conv_b, ln_g, ln_b, batch, seq):
    n = u.shape[0]
    const = lambda b: (0, 0)
    return pl.pallas_call(
        _conv_kernel,
        grid=(batch,),
        in_specs=[
            pl.BlockSpec((seq, CONV_WIDTH), lambda b: (b, 0)),
            pl.BlockSpec((CONV_KERNEL, CONV_WIDTH), const),
            pl.BlockSpec((1, CONV_WIDTH), const),
            pl.BlockSpec((1, CONV_WIDTH), const),
            pl.BlockSpec((1, CONV_WIDTH), const),
        ],
        out_specs=pl.BlockSpec((seq, CONV_WIDTH), lambda b: (b, 0)),
        out_shape=jax.ShapeDtypeStruct((n, CONV_WIDTH), BF16),
        scratch_shapes=[pltpu.VMEM((seq + 2 * CONV_HALO, CONV_WIDTH), F32)],
        compiler_params=_cparams(1, 32),
        name="conv_branch",
    )(u, conv_w, conv_b, ln_g, ln_b)


def _merge_kernel(x_ref, a_ref, c_ref, gate_ref, wup_ref, wco_ref, wout_ref, gx_ref, wxq_ref,
                  x1_ref, qx_ref):
    attn = _dot(a_ref[...], wup_ref[...])
    conv = _dot(c_ref[...], wco_ref[...])
    merged = gate_ref[:, :D_MODEL].astype(F32) * attn + gate_ref[:, D_MODEL:].astype(F32) * conv
    x1 = x_ref[...] + _dot(merged.astype(BF16), wout_ref[...])
    x1_ref[...] = x1
    h = _rms(x1, gx_ref[...]).astype(BF16)
    qx_ref[...] = (_dot(h, wxq_ref[...]) * (XHEAD_DIM ** -0.5)).astype(BF16)


def _merge(x, attn_o, conv_o, gates, w_up, w_co, w_out, g_cross, w_xq, tm):
    n = x.shape[0]
    row = lambda i: (i, 0)
    const = lambda i: (0, 0)
    return pl.pallas_call(
        _merge_kernel,
        grid=(n // tm,),
        in_specs=[
            pl.BlockSpec((tm, D_MODEL), row),
            pl.BlockSpec((tm, ATTN_WIDTH), row),
            pl.BlockSpec((tm, CONV_WIDTH), row),
            pl.BlockSpec((tm, 2 * D_MODEL), row),
            pl.BlockSpec((ATTN_WIDTH, D_MODEL), const),
            pl.BlockSpec((CONV_WIDTH, D_MODEL), const),
            pl.BlockSpec((D_MODEL, D_MODEL), const),
            pl.BlockSpec((1, D_MODEL), const),
            pl.BlockSpec((D_MODEL, D_MODEL), const),
        ],
        out_specs=[pl.BlockSpec((tm, D_MODEL), row), pl.BlockSpec((tm, D_MODEL), row)],
        out_shape=[jax.ShapeDtypeStruct((n, D_MODEL), F32), jax.ShapeDtypeStruct((n, D_MODEL), BF16)],
        compiler_params=_cparams(1, 48),
        name="merge_out_xq",
    )(x, attn_o, conv_o, gates, w_up, w_co, w_out, g_cross, w_xq)


def _memkv_kernel(m_ref, g_ref, w_ref, k_ref, v_ref):
    h = _rms(m_ref[...], g_ref[...]).astype(BF16)
    k_ref[...] = _dot(h, w_ref[:, :D_MODEL]).astype(BF16)
    v_ref[...] = _dot(h, w_ref[:, D_MODEL:]).astype(BF16)


def _mem_kv(mem, g_mem, w_xkv, tm):
    n = mem.shape[0]
    row = lambda i: (i, 0)
    const = lambda i: (0, 0)
    return pl.pallas_call(
        _memkv_kernel,
        grid=(n // tm,),
        in_specs=[
            pl.BlockSpec((tm, D_MODEL), row),
            pl.BlockSpec((1, D_MODEL), const),
            pl.BlockSpec((D_MODEL, 2 * D_MODEL), const),
        ],
        out_specs=[pl.BlockSpec((tm, D_MODEL), row), pl.BlockSpec((tm, D_MODEL), row)],
        out_shape=[jax.ShapeDtypeStruct((n, D_MODEL), BF16), jax.ShapeDtypeStruct((n, D_MODEL), BF16)],
        compiler_params=_cparams(1, 32),
        name="mem_kv",
    )(mem, g_mem, w_xkv)


def _route(lg, tri, base_ref):
    t = lg.shape[1]
    gl = lg[0:SUBLANES]
    gmax = jnp.max(gl, axis=0, keepdims=True)
    p_top = 1.0 / jnp.sum(jnp.exp(gl - gmax), axis=0, keepdims=True)
    iota_e = lax.broadcasted_iota(I32, (SUBLANES, t), 0).astype(F32)
    none = float(SUBLANES)
    grp = jnp.min(jnp.where(gl == gmax, iota_e, none), axis=0, keepdims=True)
    el = lg[SUBLANES:SUBLANES + N_EXPERTS]
    sel = el[0:EXPERTS_PER_GROUP]
    for g in range(1, N_GROUPS):
        sel = jnp.where(grp == float(g), el[g * EXPERTS_PER_GROUP:(g + 1) * EXPERTS_PER_GROUP], sel)
    v0 = jnp.max(sel, axis=0, keepdims=True)
    i0 = jnp.min(jnp.where(sel == v0, iota_e, none), axis=0, keepdims=True)
    rest = jnp.where(iota_e == i0, NEG_BIG, sel)
    v1 = jnp.max(rest, axis=0, keepdims=True)
    i1 = jnp.min(jnp.where(rest == v1, iota_e, none), axis=0, keepdims=True)
    ratio = jnp.exp(v1 - v0)
    g0 = p_top / (1.0 + ratio)
    g1 = p_top * ratio / (1.0 + ratio)
    e0 = grp * float(EXPERTS_PER_GROUP) + i0
    e1 = grp * float(EXPERTS_PER_GROUP) + i1
    iota_all = lax.broadcasted_iota(I32, (N_EXPERTS, t), 0).astype(F32)
    hit0 = iota_all == e0
    hit1 = iota_all == e1
    onehot = jnp.where(hit0 | hit1, 1.0, 0.0)
    before = _dot(onehot.astype(BF16), tri) + base_ref[:, 0:1]
    r0 = jnp.sum(jnp.where(hit0, before, 0.0), axis=0, keepdims=True)
    r1 = jnp.sum(jnp.where(hit1, before, 0.0), axis=0, keepdims=True)
    base_ref[...] = base_ref[...] + jnp.sum(onehot, axis=1, keepdims=True)
    return e0.astype(I32), e1.astype(I32), r0.astype(I32), r1.astype(I32), g0, g1


def _cross_kernel(x_ref, q_ref, k_ref, v_ref, wo_ref, gffn_ref, wrh_ref, wrl_ref, rb_ref, tri_ref,
                  x2_ref, hp_ref, idx_ref, gate_ref, cnt_ref, base_ref):
    @pl.when((pl.program_id(0) == 0) & (pl.program_id(1) == 0))
    def _():
        base_ref[...] = jnp.zeros_like(base_ref)

    heads = []
    for hd in range(N_XHEADS):
        sl = slice(hd * XHEAD_DIM, (hd + 1) * XHEAD_DIM)
        s = lax.dot_general(q_ref[:, sl], k_ref[:, sl], NT_DIMS, preferred_element_type=F32)
        m = jnp.max(s, axis=-1, keepdims=True)
        p = jnp.exp(s - m)
        l = jnp.sum(p, axis=-1, keepdims=True)
        heads.append(_dot(p.astype(BF16), v_ref[:, sl]) / l)
    o = jnp.concatenate(heads, axis=1).astype(BF16)
    x2 = x_ref[...] + _dot(o, wo_ref[...])
    x2_ref[...] = x2
    h = _rms(x2, gffn_ref[...])
    h_hi = h.astype(BF16)
    hp_ref[...] = _pack_bf16_pairs(h)
    h_lo = (h - h_hi.astype(F32)).astype(BF16)
    lg = (lax.dot_general(wrh_ref[...], h_hi, NT_DIMS, preferred_element_type=F32)
          + lax.dot_general(wrh_ref[...], h_lo, NT_DIMS, preferred_element_type=F32)
          + lax.dot_general(wrl_ref[...], h_hi, NT_DIMS, preferred_element_type=F32)) + rb_ref[:, 0:1]
    e0, e1, r0, r1, g0, g1 = _route(lg, tri_ref[...], base_ref)
    t = lg.shape[1]
    idx_ref[...] = jnp.concatenate([e0, e1, r0, r1, jnp.zeros((SUBLANES - 4, t), I32)], axis=0)
    gl = jnp.concatenate([g0, g1, jnp.zeros((LANES - 2, t), F32)], axis=0)
    gate_ref[...] = gl.T
    cnt_ref[...] = base_ref[...]


def _cross_route(x1, qx, kx, vx, w_xo, g_ffn, wr_hi, wr_lo, r_bias, tri, batch, seq, mem_len, tq):
    n = x1.shape[0]
    nq = seq // tq
    row = lambda b, i: (b * nq + i, 0)
    per_b = lambda b, i: (b, 0)
    const = lambda b, i: (0, 0)
    return pl.pallas_call(
        _cross_kernel,
        grid=(batch, nq),
        in_specs=[
            pl.BlockSpec((tq, D_MODEL), row),
            pl.BlockSpec((tq, D_MODEL), row),
            pl.BlockSpec((mem_len, D_MODEL), per_b),
            pl.BlockSpec((mem_len, D_MODEL), per_b),
            pl.BlockSpec((D_MODEL, D_MODEL), const),
            pl.BlockSpec((1, D_MODEL), const),
            pl.BlockSpec((ROUTER_ROWS, D_MODEL), const),
            pl.BlockSpec((ROUTER_ROWS, D_MODEL), const),
            pl.BlockSpec((ROUTER_ROWS, LANES), const),
            pl.BlockSpec((tq, tq), const),
        ],
        out_specs=[
            pl.BlockSpec((tq, D_MODEL), row),
            pl.BlockSpec((tq, PACKED), row),
            pl.BlockSpec((SUBLANES, tq), lambda b, i: (0, b * nq + i)),
            pl.BlockSpec((tq, LANES), row),
            pl.BlockSpec((N_EXPERTS, LANES), const),
        ],
        out_shape=[
            jax.ShapeDtypeStruct((n, D_MODEL), F32),
            jax.ShapeDtypeStruct((n, PACKED), U32),
            jax.ShapeDtypeStruct((SUBLANES, n), I32),
            jax.ShapeDtypeStruct((n, LANES), F32),
            jax.ShapeDtypeStruct((N_EXPERTS, LANES), F32),
        ],
        scratch_shapes=[pltpu.VMEM((N_EXPERTS, LANES), F32)],
        compiler_params=_cparams(2, 48),
        name="cross_attn_router",
    )(x1, qx, kx, vx, w_xo, g_ffn, wr_hi, wr_lo, r_bias, tri)


DISPATCH_TOKENS = 512
ISSUE_UNROLL = 8


def _row_copy(src_ref, src_row, dst_ref, dst_row, sem):
    return pltpu.make_async_copy(src_ref.at[pl.ds(src_row, 1)], dst_ref.at[pl.ds(dst_row, 1)], sem)


def _load_slots(slots_hbm, step, slots_smem, sem):
    cp = pltpu.make_async_copy(slots_hbm.at[step], slots_smem, sem)
    cp.start()
    cp.wait()


def _dispatch_kernel(slots_hbm, hp_hbm, xs_in, xs_out, slots_smem, idx_sem, row_sem):
    del xs_in
    step = pl.program_id(0)
    _load_slots(slots_hbm, step, slots_smem, idx_sem)
    base = step * DISPATCH_TOKENS

    def issue(r, carry):
        for k in range(2):
            slot = slots_smem[k * (SUBLANES // 2) + lax.shift_right_logical(r, 7), r & (LANES - 1)]
            _row_copy(hp_hbm, base + r, xs_out, slot, row_sem).start()
        return carry

    lax.fori_loop(0, DISPATCH_TOKENS, issue, 0, unroll=ISSUE_UNROLL)

    def drain(r, carry):
        _row_copy(hp_hbm, 0, xs_out, 0, row_sem).wait()
        return carry

    lax.fori_loop(0, 2 * DISPATCH_TOKENS, drain, 0, unroll=64)


def _dispatch(slots, hp, xs_zero):
    n = hp.shape[0]
    return pl.pallas_call(
        _dispatch_kernel,
        grid=(n // DISPATCH_TOKENS,),
        in_specs=[pl.BlockSpec(memory_space=pl.ANY)] * 3,
        out_specs=pl.BlockSpec(memory_space=pl.ANY),
        out_shape=jax.ShapeDtypeStruct(xs_zero.shape, U32),
        scratch_shapes=[pltpu.SMEM((SUBLANES, LANES), I32), pltpu.SemaphoreType.DMA, pltpu.SemaphoreType.DMA],
        input_output_aliases={2: 0},
        compiler_params=_cparams(1, 16),
        name="moe_dispatch",
    )(slots, hp, xs_zero)


def _expert_kernel(be_ref, nused_ref, xs_ref, wg_ref, wu_ref, wd_ref, ys_ref):
    del be_ref
    live = pl.program_id(0) < nused_ref[0]

    @pl.when(live)
    def _():
        x = _unpack_bf16_pairs(xs_ref[...]).astype(BF16)
        gate = _dot(x, wg_ref[...])
        hid = (gate * _sigmoid(gate)) * _dot(x, wu_ref[...])
        ys_ref[...] = _pack_bf16_pairs(_dot(hid.astype(BF16), wd_ref[...]))

    @pl.when(jnp.logical_not(live))
    def _():
        ys_ref[...] = jnp.zeros_like(ys_ref)


def _experts(block_e, n_used, xs, w_gate, w_up, w_down):
    cap = xs.shape[0]
    n_blocks = cap // EXPERT_ROWS
    rows = lambda i, be, nu: (i, 0)
    by_expert = lambda i, be, nu: (be[i], 0, 0)
    grid_spec = pltpu.PrefetchScalarGridSpec(
        num_scalar_prefetch=2,
        grid=(n_blocks,),
        in_specs=[
            pl.BlockSpec((EXPERT_ROWS, PACKED), rows),
            pl.BlockSpec((None, D_MODEL, D_FF_EXPERT), by_expert),
            pl.BlockSpec((None, D_MODEL, D_FF_EXPERT), by_expert),
            pl.BlockSpec((None, D_FF_EXPERT, D_MODEL), by_expert),
        ],
        out_specs=pl.BlockSpec((EXPERT_ROWS, PACKED), rows),
    )
    return pl.pallas_call(
        _expert_kernel,
        grid_spec=grid_spec,
        out_shape=jax.ShapeDtypeStruct((cap, PACKED), U32),
        compiler_params=_cparams(1, 32),
        name="moe_experts",
    )(block_e, n_used, xs, w_gate, w_up, w_down)


def _combine_kernel(slots_hbm, ys_hbm, x_ref, gate_ref, gfin_ref, *out_and_scratch, final_norm):
    if final_norm:
        x3_ref, y_ref, rows_ref, slots_smem, idx_sem, row_sem = out_and_scratch
    else:
        x3_ref, rows_ref, slots_smem, idx_sem, row_sem = out_and_scratch
    step = pl.program_id(0)
    _load_slots(slots_hbm, step, slots_smem, idx_sem)

    def issue(r, carry):
        for k in range(2):
            slot = slots_smem[k * (SUBLANES // 2) + lax.shift_right_logical(r, 7), r & (LANES - 1)]
            _row_copy(ys_hbm, slot, rows_ref.at[k], r, row_sem).start()
        return carry

    lax.fori_loop(0, DISPATCH_TOKENS, issue, 0, unroll=ISSUE_UNROLL)

    def drain(r, carry):
        _row_copy(ys_hbm, 0, rows_ref.at[0], 0, row_sem).wait()
        return carry

    lax.fori_loop(0, 2 * DISPATCH_TOKENS, drain, 0, unroll=64)

    y0 = _unpack_bf16_pairs(rows_ref[0])
    y1 = _unpack_bf16_pairs(rows_ref[1])
    x3 = x_ref[...] + gate_ref[:, 0:1] * y0 + gate_ref[:, 1:2] * y1
    x3_ref[...] = x3
    if final_norm:
        y_ref[...] = _rms(x3, gfin_ref[...])


def _combine(slots, ys, x2, gates, g_final, final_norm):
    n = x2.shape[0]
    row = lambda i: (i, 0)
    const = lambda i: (0, 0)
    tok = DISPATCH_TOKENS
    n_out = 2 if final_norm else 1
    out = pl.pallas_call(
        functools.partial(_combine_kernel, final_norm=final_norm),
        grid=(n // tok,),
        in_specs=[
            pl.BlockSpec(memory_space=pl.ANY),
            pl.BlockSpec(memory_space=pl.ANY),
            pl.BlockSpec((tok, D_MODEL), row),
            pl.BlockSpec((tok, LANES), row),
            pl.BlockSpec((1, D_MODEL), const),
        ],
        out_specs=[pl.BlockSpec((tok, D_MODEL), row)] * n_out,
        out_shape=[jax.ShapeDtypeStruct((n, D_MODEL), F32)] * n_out,
        scratch_shapes=[
            pltpu.VMEM((2, tok, PACKED), U32),
            pltpu.SMEM((SUBLANES, LANES), I32),
            pltpu.SemaphoreType.DMA,
            pltpu.SemaphoreType.DMA,
        ],
        compiler_params=_cparams(1, 32),
        name="moe_combine_final" if final_norm else "moe_combine",
    )(slots, ys, x2, gates, g_final)
    return out


def _rope_tables(seq):
    rows = seq // GRID_W
    row_idx = jnp.repeat(jnp.arange(rows, dtype=F32), GRID_W)
    col_idx = jnp.tile(jnp.arange(GRID_W, dtype=F32), rows)
    n_freq = HEAD_DIM // 4
    inv_freq = ROPE_THETA ** (-jnp.arange(n_freq, dtype=F32) / n_freq)
    ang_row = row_idx[:, None] * inv_freq
    ang_col = col_idx[:, None] * inv_freq
    cos = jnp.concatenate([jnp.cos(ang_row)] * 2 + [jnp.cos(ang_col)] * 2, axis=1)
    sin = jnp.concatenate([-jnp.sin(ang_row), jnp.sin(ang_row), -jnp.sin(ang_col), jnp.sin(ang_col)], axis=1)
    return jnp.tile(cos, (1, LANES // HEAD_DIM)), jnp.tile(sin, (1, LANES // HEAD_DIM))


def _segment_ones():
    head = np.arange(ATTN_WIDTH) // HEAD_DIM
    return jnp.asarray(head[:, None] == head[None, :], BF16)


def _router_weights(w_group, b_group, w_router, b_router):
    w = jnp.zeros((ROUTER_ROWS, D_MODEL), F32)
    w = w.at[0:N_GROUPS].set(w_group.T).at[SUBLANES:SUBLANES + N_EXPERTS].set(w_router.T)
    hi = w.astype(BF16)
    lo = (w - hi.astype(F32)).astype(BF16)
    b = jnp.full((ROUTER_ROWS,), NEG_BIG, F32).at[0:N_GROUPS].set(b_group)
    b = b.at[SUBLANES:SUBLANES + N_EXPERTS].set(b_router)
    return hi, lo, jnp.tile(b[:, None], (1, LANES))


def _slot_plan(idx, counts, n):
    cnt = counts[:, 0].astype(I32)
    padded = (cnt + EXPERT_ROWS - 1) // EXPERT_ROWS * EXPERT_ROWS
    pad_end = jnp.cumsum(padded)
    pad_start = pad_end - padded
    experts = jnp.arange(N_EXPERTS, dtype=I32)
    start_of = lambda e: jnp.sum(jnp.where(e[:, None] == experts[None, :], pad_start[None, :], 0), axis=1)
    slot0 = start_of(idx[0]) + idx[2]
    slot1 = start_of(idx[1]) + idx[3]
    n_tiles = n // DISPATCH_TOKENS
    per_k = SUBLANES // 2
    slots = jnp.stack([slot0, slot1]).reshape(2, n_tiles, per_k, LANES)
    slots = slots.transpose(1, 0, 2, 3).reshape(n_tiles, SUBLANES, LANES)
    n_blocks = (2 * n) // EXPERT_ROWS + N_EXPERTS
    block_first_row = jnp.arange(n_blocks, dtype=I32) * EXPERT_ROWS
    block_e = jnp.minimum(jnp.searchsorted(pad_end, block_first_row, side="right"), N_EXPERTS - 1).astype(I32)
    n_used = (pad_end[-1:] // EXPERT_ROWS).astype(I32)
    return slots, block_e, n_used, n_blocks


def kernel(x_prompt, x_sample, mem_prompt, mem_sample, g_mix, w_in, g_q, g_k, b_gate, w_attn_up, conv_w, conv_b, ln_conv_g, ln_conv_b, w_conv_out, w_out, g_cross, g_mem, w_xq, w_xkv, w_xo, g_ffn, w_group, b_group, w_router, b_router, w_e_gate, w_e_up, w_e_down, g_final):
    bp, seq, d = x_prompt.shape
    bs = x_sample.shape[0]
    assert x_sample.shape[1] == seq and d == D_MODEL and seq % GRID_W == 0
    batch = bp + bs
    mem_len = mem_prompt.shape[1]
    n = batch * seq
    depth = g_mix.shape[0]
    tm = min(512, seq)
    tq_attn = min(256, seq)
    assert seq % tm == 0 and n % DISPATCH_TOKENS == 0 and (2 * n) % EXPERT_ROWS == 0 and seq % CONV_CHUNK == 0

    x = jnp.concatenate([x_prompt.reshape(bp * seq, d), x_sample.reshape(bs * seq, d)], axis=0)
    mem = jnp.concatenate([mem_prompt.reshape(bp * mem_len, d), mem_sample.reshape(bs * mem_len, d)], axis=0)
    cos_t, sin_t = _rope_tables(seq)
    seg = _segment_ones()
    tri = jnp.asarray(np.triu(np.ones((tm, tm), np.float32), k=1), BF16)
    row2 = lambda a: a.reshape(1, -1).astype(F32)

    y = None
    for l in range(depth):
        q, k, v, u, gates = _inproj(
            x, row2(g_mix[l]), w_in[l].astype(BF16), row2(jnp.tile(g_q[l], N_HEADS)),
            row2(jnp.tile(g_k[l], N_KV_HEADS)), cos_t, sin_t, row2(b_gate[l]), seg, seq, tm)
        attn_o = _attention(q, k, v, batch, seq, tq_attn)
        conv_o = _conv_branch(u, conv_w[l], row2(conv_b[l]), row2(ln_conv_g[l]), row2(ln_conv_b[l]), batch, seq)
        x1, qx = _merge(x, attn_o, conv_o, gates, w_attn_up[l].astype(BF16), w_conv_out[l].astype(BF16),
                        w_out[l].astype(BF16), row2(g_cross[l]), w_xq[l].astype(BF16), tm)
        kx, vx = _mem_kv(mem, row2(g_mem[l]), w_xkv[l].astype(BF16), min(512, mem.shape[0]))
        wr_hi, wr_lo, r_bias = _router_weights(w_group[l], b_group[l], w_router[l], b_router[l])
        x2, hp, idx, gate_tm, counts = _cross_route(
            x1, qx, kx, vx, w_xo[l].astype(BF16), row2(g_ffn[l]), wr_hi, wr_lo, r_bias, tri,
            batch, seq, mem_len, tm)
        slots, block_e, n_used, n_blocks = _slot_plan(idx, counts, n)
        xs = _dispatch(slots, hp, jnp.zeros((n_blocks * EXPERT_ROWS, PACKED), U32))
        ys = _experts(block_e, n_used, xs, w_e_gate[l].astype(BF16), w_e_up[l].astype(BF16),
                      w_e_down[l].astype(BF16))
        last = l == depth - 1
        out = _combine(slots, ys, x2, gate_tm, row2(g_final), last)
        x = out[0]
        if last:
            y = out[1]
    y_prompt = y[:bp * seq].reshape(bp, seq, d)
    y_sample = y[bp * seq:].reshape(bs, seq, d)
    return (y_prompt, y_sample)
```

```python
import functools

import numpy as np
import jax
import jax.numpy as jnp
from jax import lax
from jax.experimental import pallas as pl
from jax.experimental.pallas import tpu as pltpu

F32 = jnp.float32
BF16 = jnp.bfloat16
U32 = jnp.uint32
I32 = jnp.int32

EPS = 1e-6
D_MODEL = 1024
GRID_W = 64
N_HEADS = 8
N_KV_HEADS = 2
HEAD_DIM = 64
ROPE_THETA = 10000.0
ATTN_WIDTH = N_HEADS * HEAD_DIM
KV_WIDTH = N_KV_HEADS * HEAD_DIM
CONV_WIDTH = 512
CONV_KERNEL = 31
N_XHEADS = 4
XHEAD_DIM = D_MODEL // N_XHEADS
N_GROUPS = 4
EXPERTS_PER_GROUP = 8
N_EXPERTS = N_GROUPS * EXPERTS_PER_GROUP
D_FF_EXPERT = 512

V7X_VMEM_BYTES = 64 * 1024 * 1024
LANES = 128
SUBLANES = 8
EXPERT_ROWS = 256
ROUTER_ROWS = 40
PACKED = D_MODEL // 2
NEG_BIG = float(np.finfo(np.float32).min)
NT_DIMS = (((1,), (1,)), ((), ()))


def _cparams(n_axes, vmem_mib):
    return pltpu.CompilerParams(
        dimension_semantics=("arbitrary",) * n_axes,
        vmem_limit_bytes=min(vmem_mib * 1024 * 1024, V7X_VMEM_BYTES - 8 * 1024 * 1024))


def _dot(a, b):
    return jnp.dot(a, b, preferred_element_type=F32)


def _sigmoid(x):
    return 1.0 / (1.0 + jnp.exp(-x))


def _rms(x, g):
    return x * lax.rsqrt(jnp.mean(x * x, axis=-1, keepdims=True) + EPS) * g


def _pack_bf16_pairs(y):
    w = y.shape[1] // 2
    yb = y.astype(BF16).astype(F32)
    hi = lax.bitcast_convert_type(yb[:, :w], U32)
    lo = lax.bitcast_convert_type(yb[:, w:], U32)
    return (hi & jnp.uint32(0xFFFF0000)) | (lo >> 16)


def _unpack_bf16_pairs(p):
    hi = lax.bitcast_convert_type(p & jnp.uint32(0xFFFF0000), F32)
    lo = lax.bitcast_convert_type(p << 16, F32)
    return jnp.concatenate([hi, lo], axis=1)


def _head_norm_rope(raw, gain, seg, cos, sin, scale):
    width = raw.shape[1]
    sq = raw * raw
    sq_hi = sq.astype(BF16)
    sq_lo = (sq - sq_hi.astype(F32)).astype(BF16)
    ss = _dot(sq_hi, seg) + _dot(sq_lo, seg)
    qn = raw * lax.rsqrt(ss * (1.0 / HEAD_DIM) + EPS) * gain
    reps = width // LANES
    cos_w = jnp.concatenate([cos] * reps, axis=1) if reps > 1 else cos
    sin_w = jnp.concatenate([sin] * reps, axis=1) if reps > 1 else sin
    lane = lax.broadcasted_iota(I32, qn.shape, 1)
    first_half = (lane & 16) == 0
    partner = jnp.where(first_half, pltpu.roll(qn, width - 16, 1), pltpu.roll(qn, 16, 1))
    return (qn * cos_w + partner * sin_w) * scale


def _select_rows(xa_ref, xb_ref, tiles_a):
    return jnp.where(pl.program_id(0) < tiles_a, xa_ref[...], xb_ref[...])


def _split_specs(tm, tiles_a):
    return [pl.BlockSpec((tm, D_MODEL), lambda i: (jnp.minimum(i, tiles_a - 1), 0)),
            pl.BlockSpec((tm, D_MODEL), lambda i: (jnp.maximum(i - tiles_a, 0), 0))]


def _inproj_kernel(xa_ref, xb_ref, gmix_ref, w_ref, gq_ref, gk_ref, cos_ref, sin_ref, bg_ref, seg_ref,
                   q_ref, k_ref, v_ref, u_ref, gate_ref, *, tiles_a):
    h = _rms(_select_rows(xa_ref, xb_ref, tiles_a), gmix_ref[...]).astype(BF16)
    cos = cos_ref[...]
    sin = sin_ref[...]
    c0 = 0
    q_raw = _dot(h, w_ref[:, c0:c0 + ATTN_WIDTH])
    q_ref[...] = _head_norm_rope(q_raw, gq_ref[...], seg_ref[...], cos, sin, HEAD_DIM ** -0.5).astype(BF16)
    c0 += ATTN_WIDTH
    k_raw = _dot(h, w_ref[:, c0:c0 + KV_WIDTH])
    k_ref[...] = _head_norm_rope(k_raw, gk_ref[...], seg_ref[:KV_WIDTH, :KV_WIDTH], cos, sin, 1.0).astype(BF16)
    c0 += KV_WIDTH
    v_ref[...] = _dot(h, w_ref[:, c0:c0 + KV_WIDTH]).astype(BF16)
    c0 += KV_WIDTH
    c_a = _dot(h, w_ref[:, c0:c0 + CONV_WIDTH])
    c0 += CONV_WIDTH
    c_b = _dot(h, w_ref[:, c0:c0 + CONV_WIDTH])
    c0 += CONV_WIDTH
    u_ref[...] = (c_a * _sigmoid(c_b)).astype(BF16)
    chunk = 512
    for j in range(2 * D_MODEL // chunk):
        lg = _dot(h, w_ref[:, c0 + j * chunk:c0 + (j + 1) * chunk]) + bg_ref[:, j * chunk:(j + 1) * chunk]
        gate_ref[:, j * chunk:(j + 1) * chunk] = _sigmoid(lg).astype(BF16)


def _inproj(xa, xb, n, gmix, w_in, gq_t, gk_t, cos_t, sin_t, b_gate, seg, seq, tm):
    tiles_a = min(xa.shape[0], n) // tm
    in_cols = w_in.shape[1]
    n_seq_tiles = seq // tm
    row = lambda i: (i, 0)
    const = lambda i: (0, 0)
    pos = lambda i: (i % n_seq_tiles, 0)
    return pl.pallas_call(
        functools.partial(_inproj_kernel, tiles_a=tiles_a),
        grid=(n // tm,),
        in_specs=_split_specs(tm, tiles_a) + [
            pl.BlockSpec((1, D_MODEL), const),
            pl.BlockSpec((D_MODEL, in_cols), const),
            pl.BlockSpec((1, ATTN_WIDTH), const),
            pl.BlockSpec((1, KV_WIDTH), const),
            pl.BlockSpec((tm, LANES), pos),
            pl.BlockSpec((tm, LANES), pos),
            pl.BlockSpec((1, 2 * D_MODEL), const),
            pl.BlockSpec((ATTN_WIDTH, ATTN_WIDTH), const),
        ],
        out_specs=[
            pl.BlockSpec((tm, ATTN_WIDTH), row),
            pl.BlockSpec((tm, KV_WIDTH), row),
            pl.BlockSpec((tm, KV_WIDTH), row),
            pl.BlockSpec((tm, CONV_WIDTH), row),
            pl.BlockSpec((tm, 2 * D_MODEL), row),
        ],
        out_shape=[
            jax.ShapeDtypeStruct((n, ATTN_WIDTH), BF16),
            jax.ShapeDtypeStruct((n, KV_WIDTH), BF16),
            jax.ShapeDtypeStruct((n, KV_WIDTH), BF16),
            jax.ShapeDtypeStruct((n, CONV_WIDTH), BF16),
            jax.ShapeDtypeStruct((n, 2 * D_MODEL), BF16),
        ],
        compiler_params=_cparams(1, 48),
        name="inproj",
    )(xa, xb, gmix, w_in, gq_t, gk_t, cos_t, sin_t, b_gate, seg)


def _attn_kernel(q_ref, k_ref, v_ref, o_ref):
    tq = q_ref.shape[0]
    group = N_HEADS // N_KV_HEADS
    outs = []
    for g in range(N_KV_HEADS):
        kg = k_ref[:, g * HEAD_DIM:(g + 1) * HEAD_DIM]
        vg = v_ref[:, g * HEAD_DIM:(g + 1) * HEAD_DIM]
        qs = jnp.concatenate(
            [q_ref[:, (group * g + h) * HEAD_DIM:(group * g + h + 1) * HEAD_DIM] for h in range(group)], axis=0)
        s = lax.dot_general(qs, kg, NT_DIMS, preferred_element_type=F32)
        m = jnp.max(s, axis=-1, keepdims=True)
        p = jnp.exp(s - m)
        l = jnp.sum(p, axis=-1, keepdims=True)
        o = _dot(p.astype(BF16), vg) / l
        outs.extend(o[h * tq:(h + 1) * tq] for h in range(group))
    o_ref[...] = jnp.concatenate(outs, axis=1).astype(BF16)


def _attention(q, k, v, batch, seq, tq):
    n = q.shape[0]
    nq = seq // tq
    return pl.pallas_call(
        _attn_kernel,
        grid=(batch, nq),
        in_specs=[
            pl.BlockSpec((tq, ATTN_WIDTH), lambda b, i: (b * nq + i, 0)),
            pl.BlockSpec((seq, KV_WIDTH), lambda b, i: (b, 0)),
            pl.BlockSpec((seq, KV_WIDTH), lambda b, i: (b, 0)),
        ],
        out_specs=pl.BlockSpec((tq, ATTN_WIDTH), lambda b, i: (b * nq + i, 0)),
        out_shape=jax.ShapeDtypeStruct((n, ATTN_WIDTH), BF16),
        compiler_params=_cparams(2, 48),
        name="gqa_attention",
    )(q, k, v)


CONV_HALO = 16
CONV_CHUNK = 64


def _conv_kernel(u_ref, w_ref, b_ref, g_ref, beta_ref, o_ref, pad_ref, sh_ref):
    seq = u_ref.shape[0]
    zeros = jnp.zeros((CONV_HALO, CONV_WIDTH), F32)
    pad_ref[0:CONV_HALO, :] = zeros
    pad_ref[CONV_HALO + seq:CONV_HALO + seq + CONV_HALO, :] = zeros
    pad_ref[CONV_HALO:CONV_HALO + seq, :] = u_ref[...].astype(F32)
    shift = CONV_HALO - CONV_KERNEL // 2
    span = CONV_CHUNK + (CONV_KERNEL // SUBLANES) * SUBLANES

    def chunk(c, carry):
        t0 = pl.multiple_of(c * CONV_CHUNK, CONV_CHUNK)
        win = pad_ref[pl.ds(t0, CONV_CHUNK + 2 * CONV_HALO), :]
        for res in range(1, SUBLANES):
            sh_ref[res - 1] = win[res:res + span]
        acc = jnp.zeros((CONV_CHUNK, CONV_WIDTH), F32)
        for tap in range(CONV_KERNEL):
            res = (tap + shift) % SUBLANES
            off = (tap + shift) - res
            if res == 0:
                rows = win[off:off + CONV_CHUNK]
            else:
                rows = sh_ref[res - 1, off:off + CONV_CHUNK, :]
            acc = acc + rows * w_ref[tap:tap + 1, :]
        acc = acc + b_ref[...]
        mu = jnp.mean(acc, axis=-1, keepdims=True)
        xc = acc - mu
        y = xc * lax.rsqrt(jnp.mean(xc * xc, axis=-1, keepdims=True) + EPS) * g_ref[...] + beta_ref[...]
        o_ref[pl.ds(t0, CONV_CHUNK), :] = (y * _sigmoid(y)).astype(BF16)
        return carry

    lax.fori_loop(0, seq // CONV_CHUNK, chunk, 0)


def _conv_branch(u, conv_w, conv_b, ln_g, ln_b, batch, seq):
    n = u.shape[0]
    const = lambda b: (0, 0)
    return pl.pallas_call(
        _conv_kernel,
        grid=(batch,),
        in_specs=[
            pl.BlockSpec((seq, CONV_WIDTH), lambda b: (b, 0)),
            pl.BlockSpec((CONV_KERNEL, CONV_WIDTH), const),
            pl.BlockSpec((1, CONV_WIDTH), const),
            pl.BlockSpec((1, CONV_WIDTH), const),
            pl.BlockSpec((1, CONV_WIDTH), const),
        ],
        out_specs=pl.BlockSpec((seq, CONV_WIDTH), lambda b: (b, 0)),
        out_shape=jax.ShapeDtypeStruct((n, CONV_WIDTH), BF16),
        scratch_shapes=[
            pltpu.VMEM((seq + 2 * CONV_HALO, CONV_WIDTH), F32),
            pltpu.VMEM((SUBLANES - 1, CONV_CHUNK + (CONV_KERNEL // SUBLANES) * SUBLANES, CONV_WIDTH), F32),
        ],
        compiler_params=_cparams(1, 32),
        name="conv_branch",
    )(u, conv_w, conv_b, ln_g, ln_b)


def _merge_kernel(xa_ref, xb_ref, a_ref, c_ref, gate_ref, wup_ref, wco_ref, wout_ref, gx_ref, wxq_ref,
                  x1_ref, qx_ref, *, tiles_a):
    attn = _dot(a_ref[...], wup_ref[...])
    conv = _dot(c_ref[...], wco_ref[...])
    merged = gate_ref[:, :D_MODEL].astype(F32) * attn + gate_ref[:, D_MODEL:].astype(F32) * conv
    x1 = _select_rows(xa_ref, xb_ref, tiles_a) + _dot(merged.astype(BF16), wout_ref[...])
    x1_ref[...] = x1
    h = _rms(x1, gx_ref[...]).astype(BF16)
    qx_ref[...] = (_dot(h, wxq_ref[...]) * (XHEAD_DIM ** -0.5)).astype(BF16)


def _merge(xa, xb, n, attn_o, conv_o, gates, w_up, w_co, w_out, g_cross, w_xq, tm):
    tiles_a = min(xa.shape[0], n) // tm
    row = lambda i: (i, 0)
    const = lambda i: (0, 0)
    return pl.pallas_call(
        functools.partial(_merge_kernel, tiles_a=tiles_a),
        grid=(n // tm,),
        in_specs=_split_specs(tm, tiles_a) + [
            pl.BlockSpec((tm, ATTN_WIDTH), row),
            pl.BlockSpec((tm, CONV_WIDTH), row),
            pl.BlockSpec((tm, 2 * D_MODEL), row),
            pl.BlockSpec((ATTN_WIDTH, D_MODEL), const),
            pl.BlockSpec((CONV_WIDTH, D_MODEL), const),
            pl.BlockSpec((D_MODEL, D_MODEL), const),
            pl.BlockSpec((1, D_MODEL), const),
            pl.BlockSpec((D_MODEL, D_MODEL), const),
        ],
        out_specs=[pl.BlockSpec((tm, D_MODEL), row), pl.BlockSpec((tm, D_MODEL), row)],
        out_shape=[jax.ShapeDtypeStruct((n, D_MODEL), F32), jax.ShapeDtypeStruct((n, D_MODEL), BF16)],
        compiler_params=_cparams(1, 48),
        name="merge_out_xq",
    )(xa, xb, attn_o, conv_o, gates, w_up, w_co, w_out, g_cross, w_xq)


def _memkv_kernel(m_ref, g_ref, w_ref, k_ref, v_ref):
    h = _rms(m_ref[...], g_ref[...]).astype(BF16)
    k_ref[...] = _dot(h, w_ref[:, :D_MODEL]).astype(BF16)
    v_ref[...] = _dot(h, w_ref[:, D_MODEL:]).astype(BF16)


def _mem_kv(mem, g_mem, w_xkv, tm):
    n = mem.shape[0]
    row = lambda i: (i, 0)
    const = lambda i: (0, 0)
    return pl.pallas_call(
        _memkv_kernel,
        grid=(n // tm,),
        in_specs=[
            pl.BlockSpec((tm, D_MODEL), row),
            pl.BlockSpec((1, D_MODEL), const),
            pl.BlockSpec((D_MODEL, 2 * D_MODEL), const),
        ],
        out_specs=[pl.BlockSpec((tm, D_MODEL), row), pl.BlockSpec((tm, D_MODEL), row)],
        out_shape=[jax.ShapeDtypeStruct((n, D_MODEL), BF16), jax.ShapeDtypeStruct((n, D_MODEL), BF16)],
        compiler_params=_cparams(1, 32),
        name="mem_kv",
    )(mem, g_mem, w_xkv)


def _route(lg, tri, base_ref):
    t = lg.shape[1]
    gl = lg[0:SUBLANES]
    gmax = jnp.max(gl, axis=0, keepdims=True)
    p_top = 1.0 / jnp.sum(jnp.exp(gl - gmax), axis=0, keepdims=True)
    iota_e = lax.broadcasted_iota(I32, (SUBLANES, t), 0).astype(F32)
    none = float(SUBLANES)
    grp = jnp.min(jnp.where(gl == gmax, iota_e, none), axis=0, keepdims=True)
    el = lg[SUBLANES:SUBLANES + N_EXPERTS]
    sel = el[0:EXPERTS_PER_GROUP]
    for g in range(1, N_GROUPS):
        sel = jnp.where(grp == float(g), el[g * EXPERTS_PER_GROUP:(g + 1) * EXPERTS_PER_GROUP], sel)
    v0 = jnp.max(sel, axis=0, keepdims=True)
    i0 = jnp.min(jnp.where(sel == v0, iota_e, none), axis=0, keepdims=True)
    rest = jnp.where(iota_e == i0, NEG_BIG, sel)
    v1 = jnp.max(rest, axis=0, keepdims=True)
    i1 = jnp.min(jnp.where(rest == v1, iota_e, none), axis=0, keepdims=True)
    ratio = jnp.exp(v1 - v0)
    g0 = p_top / (1.0 + ratio)
    g1 = p_top * ratio / (1.0 + ratio)
    e0 = grp * float(EXPERTS_PER_GROUP) + i0
    e1 = grp * float(EXPERTS_PER_GROUP) + i1
    iota_all = lax.broadcasted_iota(I32, (N_EXPERTS, t), 0).astype(F32)
    hit0 = iota_all == e0
    hit1 = iota_all == e1
    onehot = jnp.where(hit0 | hit1, 1.0, 0.0)
    before = _dot(onehot.astype(BF16), tri) + base_ref[:, 0:1]
    r0 = jnp.sum(jnp.where(hit0, before, 0.0), axis=0, keepdims=True)
    r1 = jnp.sum(jnp.where(hit1, before, 0.0), axis=0, keepdims=True)
    base_ref[...] = base_ref[...] + jnp.sum(onehot, axis=1, keepdims=True)
    return e0.astype(I32), e1.astype(I32), r0.astype(I32), r1.astype(I32), g0, g1


def _cross_kernel(x_ref, q_ref, k_ref, v_ref, wo_ref, gffn_ref, wrh_ref, wrl_ref, rb_ref, tri_ref,
                  x2_ref, hp_ref, idx_ref, gate_ref, cnt_ref, base_ref):
    @pl.when((pl.program_id(0) == 0) & (pl.program_id(1) == 0))
    def _():
        base_ref[...] = jnp.zeros_like(base_ref)

    heads = []
    for hd in range(N_XHEADS):
        sl = slice(hd * XHEAD_DIM, (hd + 1) * XHEAD_DIM)
        s = lax.dot_general(q_ref[:, sl], k_ref[:, sl], NT_DIMS, preferred_element_type=F32)
        m = jnp.max(s, axis=-1, keepdims=True)
        p = jnp.exp(s - m)
        l = jnp.sum(p, axis=-1, keepdims=True)
        heads.append(_dot(p.astype(BF16), v_ref[:, sl]) / l)
    o = jnp.concatenate(heads, axis=1).astype(BF16)
    x2 = x_ref[...] + _dot(o, wo_ref[...])
    x2_ref[...] = x2
    h = _rms(x2, gffn_ref[...])
    h_hi = h.astype(BF16)
    hp_ref[...] = _pack_bf16_pairs(h)
    h_lo = (h - h_hi.astype(F32)).astype(BF16)
    lg = (lax.dot_general(wrh_ref[...], h_hi, NT_DIMS, preferred_element_type=F32)
          + lax.dot_general(wrh_ref[...], h_lo, NT_DIMS, preferred_element_type=F32)
          + lax.dot_general(wrl_ref[...], h_hi, NT_DIMS, preferred_element_type=F32)) + rb_ref[:, 0:1]
    e0, e1, r0, r1, g0, g1 = _route(lg, tri_ref[...], base_ref)
    t = lg.shape[1]
    idx_ref[...] = jnp.concatenate([e0, e1, r0, r1, jnp.zeros((SUBLANES - 4, t), I32)], axis=0)
    gl = jnp.concatenate([g0, g1, jnp.zeros((LANES - 2, t), F32)], axis=0)
    gate_ref[...] = gl.T
    cnt_ref[...] = base_ref[...]


def _cross_route(x1, qx, kx, vx, w_xo, g_ffn, wr_hi, wr_lo, r_bias, tri, batch, seq, mem_len, tq):
    n = x1.shape[0]
    nq = seq // tq
    row = lambda b, i: (b * nq + i, 0)
    per_b = lambda b, i: (b, 0)
    const = lambda b, i: (0, 0)
    return pl.pallas_call(
        _cross_kernel,
        grid=(batch, nq),
        in_specs=[
            pl.BlockSpec((tq, D_MODEL), row),
            pl.BlockSpec((tq, D_MODEL), row),
            pl.BlockSpec((mem_len, D_MODEL), per_b),
            pl.BlockSpec((mem_len, D_MODEL), per_b),
            pl.BlockSpec((D_MODEL, D_MODEL), const),
            pl.BlockSpec((1, D_MODEL), const),
            pl.BlockSpec((ROUTER_ROWS, D_MODEL), const),
            pl.BlockSpec((ROUTER_ROWS, D_MODEL), const),
            pl.BlockSpec((ROUTER_ROWS, LANES), const),
            pl.BlockSpec((tq, tq), const),
        ],
        out_specs=[
            pl.BlockSpec((tq, D_MODEL), row),
            pl.BlockSpec((tq, PACKED), row),
            pl.BlockSpec((SUBLANES, tq), lambda b, i: (0, b * nq + i)),
            pl.BlockSpec((tq, LANES), row),
            pl.BlockSpec((N_EXPERTS, LANES), const),
        ],
        out_shape=[
            jax.ShapeDtypeStruct((n, D_MODEL), F32),
            jax.ShapeDtypeStruct((n, PACKED), U32),
            jax.ShapeDtypeStruct((SUBLANES, n), I32),
            jax.ShapeDtypeStruct((n, LANES), F32),
            jax.ShapeDtypeStruct((N_EXPERTS, LANES), F32),
        ],
        scratch_shapes=[pltpu.VMEM((N_EXPERTS, LANES), F32)],
        compiler_params=_cparams(2, 48),
        name="cross_attn_router",
    )(x1, qx, kx, vx, w_xo, g_ffn, wr_hi, wr_lo, r_bias, tri)


DISPATCH_TOKENS = 512
ISSUE_UNROLL = 8


def _row_copy(src_ref, src_row, dst_ref, dst_row, sem):
    return pltpu.make_async_copy(src_ref.at[pl.ds(src_row, 1)], dst_ref.at[pl.ds(dst_row, 1)], sem)


def _load_slots(slots_hbm, step, slots_smem, sem):
    cp = pltpu.make_async_copy(slots_hbm.at[step], slots_smem, sem)
    cp.start()
    cp.wait()


def _dispatch_kernel(slots_hbm, hp_ref, xs_in, xs_out, slots_smem, idx_sem, row_sem):
    del xs_in
    _load_slots(slots_hbm, pl.program_id(0), slots_smem, idx_sem)

    def issue(group, c, carry):
        for k in range(2):
            slot = slots_smem[k * (SUBLANES // 2) + group, c]
            _row_copy(hp_ref, group * LANES + c, xs_out, slot, row_sem).start(priority=k)
        return carry

    for group in range(DISPATCH_TOKENS // LANES):
        lax.fori_loop(0, LANES, functools.partial(issue, group), 0, unroll=ISSUE_UNROLL)

    def drain(r, carry):
        _row_copy(hp_ref, 0, xs_out, 0, row_sem).wait()
        return carry

    lax.fori_loop(0, 2 * DISPATCH_TOKENS, drain, 0, unroll=64)


def _dispatch(slots, hp, xs_zero):
    n = hp.shape[0]
    return pl.pallas_call(
        _dispatch_kernel,
        grid=(n // DISPATCH_TOKENS,),
        in_specs=[
            pl.BlockSpec(memory_space=pl.ANY),
            pl.BlockSpec((DISPATCH_TOKENS, PACKED), lambda i: (i, 0)),
            pl.BlockSpec(memory_space=pl.ANY),
        ],
        out_specs=pl.BlockSpec(memory_space=pl.ANY),
        out_shape=jax.ShapeDtypeStruct(xs_zero.shape, U32),
        scratch_shapes=[pltpu.SMEM((SUBLANES, LANES), I32), pltpu.SemaphoreType.DMA, pltpu.SemaphoreType.DMA],
        input_output_aliases={2: 0},
        compiler_params=_cparams(1, 16),
        name="moe_dispatch",
    )(slots, hp, xs_zero)


def _expert_kernel(be_ref, nused_ref, xs_ref, wg_ref, wu_ref, wd_ref, ys_ref):
    del be_ref
    live = pl.program_id(0) < nused_ref[0]

    @pl.when(live)
    def _():
        x = _unpack_bf16_pairs(xs_ref[...]).astype(BF16)
        gate = _dot(x, wg_ref[...])
        hid = (gate * _sigmoid(gate)) * _dot(x, wu_ref[...])
        ys_ref[...] = _pack_bf16_pairs(_dot(hid.astype(BF16), wd_ref[...]))

    @pl.when(jnp.logical_not(live))
    def _():
        ys_ref[...] = jnp.zeros_like(ys_ref)


def _experts(block_e, n_used, xs, w_gate, w_up, w_down):
    cap = xs.shape[0]
    n_blocks = cap // EXPERT_ROWS
    rows = lambda i, be, nu: (i, 0)
    by_expert = lambda i, be, nu: (be[i], 0, 0)
    grid_spec = pltpu.PrefetchScalarGridSpec(
        num_scalar_prefetch=2,
        grid=(n_blocks,),
        in_specs=[
            pl.BlockSpec((EXPERT_ROWS, PACKED), rows),
            pl.BlockSpec((None, D_MODEL, D_FF_EXPERT), by_expert),
            pl.BlockSpec((None, D_MODEL, D_FF_EXPERT), by_expert),
            pl.BlockSpec((None, D_FF_EXPERT, D_MODEL), by_expert),
        ],
        out_specs=pl.BlockSpec((EXPERT_ROWS, PACKED), rows),
    )
    return pl.pallas_call(
        _expert_kernel,
        grid_spec=grid_spec,
        out_shape=jax.ShapeDtypeStruct((cap, PACKED), U32),
        compiler_params=_cparams(1, 32),
        name="moe_experts",
    )(block_e, n_used, xs, w_gate, w_up, w_down)


def _combine_kernel(slots_hbm, ys_hbm, x_ref, gate_ref, gfin_ref, *out_and_scratch, tiles_a):
    if tiles_a is None:
        x3_ref, rows_ref, slots_smem, idx_sem, row_sem = out_and_scratch
    else:
        ya_ref, yb_ref, rows_ref, slots_smem, idx_sem, row_sem = out_and_scratch
    step = pl.program_id(0)
    _load_slots(slots_hbm, step, slots_smem, idx_sem)

    def issue(group, c, carry):
        for k in range(2):
            slot = slots_smem[k * (SUBLANES // 2) + group, c]
            _row_copy(ys_hbm, slot, rows_ref.at[k], group * LANES + c, row_sem).start(priority=k)
        return carry

    for group in range(DISPATCH_TOKENS // LANES):
        lax.fori_loop(0, LANES, functools.partial(issue, group), 0, unroll=ISSUE_UNROLL)

    def drain(r, carry):
        _row_copy(ys_hbm, 0, rows_ref.at[0], 0, row_sem).wait()
        return carry

    lax.fori_loop(0, 2 * DISPATCH_TOKENS, drain, 0, unroll=64)

    y0 = _unpack_bf16_pairs(rows_ref[0])
    y1 = _unpack_bf16_pairs(rows_ref[1])
    x3 = x_ref[...] + gate_ref[:, 0:1] * y0 + gate_ref[:, 1:2] * y1
    if tiles_a is None:
        x3_ref[...] = x3
    else:
        y = _rms(x3, gfin_ref[...])

        @pl.when(step < tiles_a)
        def _():
            ya_ref[...] = y

        @pl.when(step >= tiles_a)
        def _():
            yb_ref[...] = y


def _combine(slots, ys, x2, gates, g_final, rows_a):
    n = x2.shape[0]
    row = lambda i: (i, 0)
    const = lambda i: (0, 0)
    tok = DISPATCH_TOKENS
    if rows_a is None:
        tiles_a = None
        out_specs = [pl.BlockSpec((tok, D_MODEL), row)]
        out_shape = [jax.ShapeDtypeStruct((n, D_MODEL), F32)]
    else:
        tiles_a = rows_a // tok
        out_specs = [pl.BlockSpec((tok, D_MODEL), lambda i: (jnp.minimum(i, tiles_a - 1), 0)),
                     pl.BlockSpec((tok, D_MODEL), lambda i: (jnp.maximum(i - tiles_a, 0), 0))]
        out_shape = [jax.ShapeDtypeStruct((rows_a, D_MODEL), F32), jax.ShapeDtypeStruct((n - rows_a, D_MODEL), F32)]
    out = pl.pallas_call(
        functools.partial(_combine_kernel, tiles_a=tiles_a),
        grid=(n // tok,),
        in_specs=[
            pl.BlockSpec(memory_space=pl.ANY),
            pl.BlockSpec(memory_space=pl.ANY),
            pl.BlockSpec((tok, D_MODEL), row),
            pl.BlockSpec((tok, LANES), row),
            pl.BlockSpec((1, D_MODEL), const),
        ],
        out_specs=out_specs,
        out_shape=out_shape,
        scratch_shapes=[
            pltpu.VMEM((2, tok, PACKED), U32),
            pltpu.SMEM((SUBLANES, LANES), I32),
            pltpu.SemaphoreType.DMA,
            pltpu.SemaphoreType.DMA,
        ],
        compiler_params=_cparams(1, 32),
        name="moe_combine" if rows_a is None else "moe_combine_final",
    )(slots, ys, x2, gates, g_final)
    return out


def _rope_tables(seq):
    rows = seq // GRID_W
    row_idx = jnp.repeat(jnp.arange(rows, dtype=F32), GRID_W)
    col_idx = jnp.tile(jnp.arange(GRID_W, dtype=F32), rows)
    n_freq = HEAD_DIM // 4
    inv_freq = ROPE_THETA ** (-jnp.arange(n_freq, dtype=F32) / n_freq)
    ang_row = row_idx[:, None] * inv_freq
    ang_col = col_idx[:, None] * inv_freq
    cos = jnp.concatenate([jnp.cos(ang_row)] * 2 + [jnp.cos(ang_col)] * 2, axis=1)
    sin = jnp.concatenate([-jnp.sin(ang_row), jnp.sin(ang_row), -jnp.sin(ang_col), jnp.sin(ang_col)], axis=1)
    return jnp.tile(cos, (1, LANES // HEAD_DIM)), jnp.tile(sin, (1, LANES // HEAD_DIM))


def _segment_ones():
    head = np.arange(ATTN_WIDTH) // HEAD_DIM
    return jnp.asarray(head[:, None] == head[None, :], BF16)


def _router_weights(w_group, b_group, w_router, b_router):
    w = jnp.zeros((ROUTER_ROWS, D_MODEL), F32)
    w = w.at[0:N_GROUPS].set(w_group.T).at[SUBLANES:SUBLANES + N_EXPERTS].set(w_router.T)
    hi = w.astype(BF16)
    lo = (w - hi.astype(F32)).astype(BF16)
    b = jnp.full((ROUTER_ROWS,), NEG_BIG, F32).at[0:N_GROUPS].set(b_group)
    b = b.at[SUBLANES:SUBLANES + N_EXPERTS].set(b_router)
    return hi, lo, jnp.tile(b[:, None], (1, LANES))


def _slot_plan(idx, counts, n):
    cnt = counts[:, 0].astype(I32)
    padded = (cnt + EXPERT_ROWS - 1) // EXPERT_ROWS * EXPERT_ROWS
    pad_end = jnp.cumsum(padded)
    pad_start = pad_end - padded
    experts = jnp.arange(N_EXPERTS, dtype=I32)
    start_of = lambda e: jnp.sum(jnp.where(e[:, None] == experts[None, :], pad_start[None, :], 0), axis=1)
    slot0 = start_of(idx[0]) + idx[2]
    slot1 = start_of(idx[1]) + idx[3]
    n_tiles = n // DISPATCH_TOKENS
    per_k = SUBLANES // 2
    slots = jnp.stack([slot0, slot1]).reshape(2, n_tiles, per_k, LANES)
    slots = slots.transpose(1, 0, 2, 3).reshape(n_tiles, SUBLANES, LANES)
    n_blocks = (2 * n) // EXPERT_ROWS + N_EXPERTS
    block_first_row = jnp.arange(n_blocks, dtype=I32) * EXPERT_ROWS
    ends_passed = jnp.sum((pad_end[None, :] <= block_first_row[:, None]).astype(I32), axis=1)
    block_e = jnp.minimum(ends_passed, N_EXPERTS - 1)
    n_used = (pad_end[-1:] // EXPERT_ROWS).astype(I32)
    return slots, block_e, n_used, n_blocks


def kernel(x_prompt, x_sample, mem_prompt, mem_sample, g_mix, w_in, g_q, g_k, b_gate, w_attn_up, conv_w, conv_b, ln_conv_g, ln_conv_b, w_conv_out, w_out, g_cross, g_mem, w_xq, w_xkv, w_xo, g_ffn, w_group, b_group, w_router, b_router, w_e_gate, w_e_up, w_e_down, g_final):
    bp, seq, d = x_prompt.shape
    bs = x_sample.shape[0]
    assert x_sample.shape[1] == seq and d == D_MODEL and seq % GRID_W == 0
    batch = bp + bs
    mem_len = mem_prompt.shape[1]
    n = batch * seq
    depth = g_mix.shape[0]
    tm = min(512, seq)
    tq_attn = min(256, seq)
    assert seq % tm == 0 and n % DISPATCH_TOKENS == 0 and (2 * n) % EXPERT_ROWS == 0 and seq % CONV_CHUNK == 0
    assert (bp * seq) % DISPATCH_TOKENS == 0

    xa = x_prompt.reshape(bp * seq, d)
    xb = x_sample.reshape(bs * seq, d)
    mem = jnp.concatenate([mem_prompt.reshape(bp * mem_len, d), mem_sample.reshape(bs * mem_len, d)], axis=0)
    cos_t, sin_t = _rope_tables(seq)
    seg = _segment_ones()
    tri = jnp.asarray(np.triu(np.ones((tm, tm), np.float32), k=1), BF16)
    row2 = lambda a: a.reshape(1, -1).astype(F32)

    out = None
    for l in range(depth):
        q, k, v, u, gates = _inproj(
            xa, xb, n, row2(g_mix[l]), w_in[l].astype(BF16), row2(jnp.tile(g_q[l], N_HEADS)),
            row2(jnp.tile(g_k[l], N_KV_HEADS)), cos_t, sin_t, row2(b_gate[l]), seg, seq, tm)
        attn_o = _attention(q, k, v, batch, seq, tq_attn)
        conv_o = _conv_branch(u, conv_w[l], row2(conv_b[l]), row2(ln_conv_g[l]), row2(ln_conv_b[l]), batch, seq)
        x1, qx = _merge(xa, xb, n, attn_o, conv_o, gates, w_attn_up[l].astype(BF16), w_conv_out[l].astype(BF16),
                        w_out[l].astype(BF16), row2(g_cross[l]), w_xq[l].astype(BF16), tm)
        kx, vx = _mem_kv(mem, row2(g_mem[l]), w_xkv[l].astype(BF16), min(512, mem.shape[0]))
        wr_hi, wr_lo, r_bias = _router_weights(w_group[l], b_group[l], w_router[l], b_router[l])
        x2, hp, idx, gate_tm, counts = _cross_route(
            x1, qx, kx, vx, w_xo[l].astype(BF16), row2(g_ffn[l]), wr_hi, wr_lo, r_bias, tri,
            batch, seq, mem_len, tm)
        slots, block_e, n_used, n_blocks = _slot_plan(idx, counts, n)
        xs = _dispatch(slots, hp, jnp.zeros((n_blocks * EXPERT_ROWS, PACKED), U32))
        ys = _experts(block_e, n_used, xs, w_e_gate[l].astype(BF16), w_e_up[l].astype(BF16),
                      w_e_down[l].astype(BF16))
        last = l == depth - 1
        out = _combine(slots, ys, x2, gate_tm, row2(g_final), bp * seq if last else None)
        xa = xb = out[0]
    return (out[0].reshape(bp, seq, d), out[1].reshape(bs, seq, d))
```

```python
import functools

import numpy as np
import jax
import jax.numpy as jnp
from jax import lax
from jax.experimental import pallas as pl
from jax.experimental.pallas import tpu as pltpu

F32 = jnp.float32
BF16 = jnp.bfloat16
U32 = jnp.uint32
I32 = jnp.int32

EPS = 1e-6
D_MODEL = 1024
GRID_W = 64
N_HEADS = 8
N_KV_HEADS = 2
HEAD_DIM = 64
ROPE_THETA = 10000.0
ATTN_WIDTH = N_HEADS * HEAD_DIM
KV_WIDTH = N_KV_HEADS * HEAD_DIM
CONV_WIDTH = 512
CONV_KERNEL = 31
N_XHEADS = 4
XHEAD_DIM = D_MODEL // N_XHEADS
N_GROUPS = 4
EXPERTS_PER_GROUP = 8
N_EXPERTS = N_GROUPS * EXPERTS_PER_GROUP
D_FF_EXPERT = 512

V7X_VMEM_BYTES = 64 * 1024 * 1024
LANES = 128
SUBLANES = 8
EXPERT_ROWS = 256
ROUTER_ROWS = 40
PACKED = D_MODEL // 2
NEG_BIG = float(np.finfo(np.float32).min)
NT_DIMS = (((1,), (1,)), ((), ()))


def _cparams(n_axes, vmem_mib):
    return pltpu.CompilerParams(
        dimension_semantics=("arbitrary",) * n_axes,
        vmem_limit_bytes=min(vmem_mib * 1024 * 1024, V7X_VMEM_BYTES - 8 * 1024 * 1024))


def _dot(a, b):
    return jnp.dot(a, b, preferred_element_type=F32)


def _sigmoid(x):
    return 1.0 / (1.0 + jnp.exp(-x))


def _rms(x, g):
    return x * lax.rsqrt(jnp.mean(x * x, axis=-1, keepdims=True) + EPS) * g


def _pack_bf16_pairs(y):
    w = y.shape[1] // 2
    yb = y.astype(BF16).astype(F32)
    hi = lax.bitcast_convert_type(yb[:, :w], U32)
    lo = lax.bitcast_convert_type(yb[:, w:], U32)
    return (hi & jnp.uint32(0xFFFF0000)) | (lo >> 16)


def _unpack_bf16_pairs(p):
    hi = lax.bitcast_convert_type(p & jnp.uint32(0xFFFF0000), F32)
    lo = lax.bitcast_convert_type(p << 16, F32)
    return jnp.concatenate([hi, lo], axis=1)


ROW_TILES = PACKED // LANES


def _store_token_rows(ref, packed):
    t = packed.shape[0]
    for j in range(ROW_TILES):
        ref[pl.ds(j, t, stride=ROW_TILES), :] = packed[:, j * LANES:(j + 1) * LANES]


def _load_token_rows(ref):
    t = ref.shape[0] // ROW_TILES
    return jnp.concatenate([ref[pl.ds(j, t, stride=ROW_TILES), :] for j in range(ROW_TILES)], axis=1)


def _head_norm_rope(raw, gain, seg, cos, sin, scale):
    width = raw.shape[1]
    sq = raw * raw
    sq_hi = sq.astype(BF16)
    sq_lo = (sq - sq_hi.astype(F32)).astype(BF16)
    ss = _dot(sq_hi, seg) + _dot(sq_lo, seg)
    qn = raw * lax.rsqrt(ss * (1.0 / HEAD_DIM) + EPS) * gain
    reps = width // LANES
    cos_w = jnp.concatenate([cos] * reps, axis=1) if reps > 1 else cos
    sin_w = jnp.concatenate([sin] * reps, axis=1) if reps > 1 else sin
    lane = lax.broadcasted_iota(I32, qn.shape, 1)
    first_half = (lane & 16) == 0
    partner = jnp.where(first_half, pltpu.roll(qn, width - 16, 1), pltpu.roll(qn, 16, 1))
    return (qn * cos_w + partner * sin_w) * scale


def _select_rows(xa_ref, xb_ref, tiles_a):
    return jnp.where(pl.program_id(0) < tiles_a, xa_ref[...], xb_ref[...])


def _split_specs(tm, tiles_a):
    return [pl.BlockSpec((tm, D_MODEL), lambda i: (jnp.minimum(i, tiles_a - 1), 0)),
            pl.BlockSpec((tm, D_MODEL), lambda i: (jnp.maximum(i - tiles_a, 0), 0))]


def _inproj_kernel(xa_ref, xb_ref, gmix_ref, w_ref, gq_ref, gk_ref, cos_ref, sin_ref, bg_ref, seg_ref,
                   q_ref, k_ref, v_ref, u_ref, gate_ref, *, tiles_a):
    h = _rms(_select_rows(xa_ref, xb_ref, tiles_a), gmix_ref[...]).astype(BF16)
    cos = cos_ref[...]
    sin = sin_ref[...]
    c0 = 0
    q_raw = _dot(h, w_ref[:, c0:c0 + ATTN_WIDTH])
    q_ref[...] = _head_norm_rope(q_raw, gq_ref[...], seg_ref[...], cos, sin, HEAD_DIM ** -0.5).astype(BF16)
    c0 += ATTN_WIDTH
    k_raw = _dot(h, w_ref[:, c0:c0 + KV_WIDTH])
    k_ref[...] = _head_norm_rope(k_raw, gk_ref[...], seg_ref[:KV_WIDTH, :KV_WIDTH], cos, sin, 1.0).astype(BF16)
    c0 += KV_WIDTH
    v_ref[...] = _dot(h, w_ref[:, c0:c0 + KV_WIDTH]).astype(BF16)
    c0 += KV_WIDTH
    c_a = _dot(h, w_ref[:, c0:c0 + CONV_WIDTH])
    c0 += CONV_WIDTH
    c_b = _dot(h, w_ref[:, c0:c0 + CONV_WIDTH])
    c0 += CONV_WIDTH
    u_ref[...] = (c_a * _sigmoid(c_b)).astype(BF16)
    chunk = 512
    for j in range(2 * D_MODEL // chunk):
        lg = _dot(h, w_ref[:, c0 + j * chunk:c0 + (j + 1) * chunk]) + bg_ref[:, j * chunk:(j + 1) * chunk]
        gate_ref[:, j * chunk:(j + 1) * chunk] = _sigmoid(lg).astype(BF16)


def _inproj(xa, xb, n, gmix, w_in, gq_t, gk_t, cos_t, sin_t, b_gate, seg, seq, tm):
    tiles_a = min(xa.shape[0], n) // tm
    in_cols = w_in.shape[1]
    n_seq_tiles = seq // tm
    row = lambda i: (i, 0)
    const = lambda i: (0, 0)
    pos = lambda i: (i % n_seq_tiles, 0)
    return pl.pallas_call(
        functools.partial(_inproj_kernel, tiles_a=tiles_a),
        grid=(n // tm,),
        in_specs=_split_specs(tm, tiles_a) + [
            pl.BlockSpec((1, D_MODEL), const),
            pl.BlockSpec((D_MODEL, in_cols), const),
            pl.BlockSpec((1, ATTN_WIDTH), const),
            pl.BlockSpec((1, KV_WIDTH), const),
            pl.BlockSpec((tm, LANES), pos),
            pl.BlockSpec((tm, LANES), pos),
            pl.BlockSpec((1, 2 * D_MODEL), const),
            pl.BlockSpec((ATTN_WIDTH, ATTN_WIDTH), const),
        ],
        out_specs=[
            pl.BlockSpec((tm, ATTN_WIDTH), row),
            pl.BlockSpec((tm, KV_WIDTH), row),
            pl.BlockSpec((tm, KV_WIDTH), row),
            pl.BlockSpec((tm, CONV_WIDTH), row),
            pl.BlockSpec((tm, 2 * D_MODEL), row),
        ],
        out_shape=[
            jax.ShapeDtypeStruct((n, ATTN_WIDTH), BF16),
            jax.ShapeDtypeStruct((n, KV_WIDTH), BF16),
            jax.ShapeDtypeStruct((n, KV_WIDTH), BF16),
            jax.ShapeDtypeStruct((n, CONV_WIDTH), BF16),
            jax.ShapeDtypeStruct((n, 2 * D_MODEL), BF16),
        ],
        compiler_params=_cparams(1, 48),
        name="inproj",
    )(xa, xb, gmix, w_in, gq_t, gk_t, cos_t, sin_t, b_gate, seg)


def _attn_kernel(q_ref, k_ref, v_ref, o_ref):
    group = N_HEADS // N_KV_HEADS
    outs = []
    for hd in range(N_HEADS):
        g = hd // group
        kg = k_ref[:, g * HEAD_DIM:(g + 1) * HEAD_DIM]
        vg = v_ref[:, g * HEAD_DIM:(g + 1) * HEAD_DIM]
        s = lax.dot_general(q_ref[:, hd * HEAD_DIM:(hd + 1) * HEAD_DIM], kg, NT_DIMS, preferred_element_type=F32)
        m = jnp.max(s, axis=-1, keepdims=True)
        p = jnp.exp(s - m)
        l = jnp.sum(p, axis=-1, keepdims=True)
        outs.append(_dot(p.astype(BF16), vg) / l)
    o_ref[...] = jnp.concatenate(outs, axis=1).astype(BF16)


def _attention(q, k, v, batch, seq, tq):
    n = q.shape[0]
    nq = seq // tq
    return pl.pallas_call(
        _attn_kernel,
        grid=(batch, nq),
        in_specs=[
            pl.BlockSpec((tq, ATTN_WIDTH), lambda b, i: (b * nq + i, 0)),
            pl.BlockSpec((seq, KV_WIDTH), lambda b, i: (b, 0)),
            pl.BlockSpec((seq, KV_WIDTH), lambda b, i: (b, 0)),
        ],
        out_specs=pl.BlockSpec((tq, ATTN_WIDTH), lambda b, i: (b * nq + i, 0)),
        out_shape=jax.ShapeDtypeStruct((n, ATTN_WIDTH), BF16),
        compiler_params=_cparams(2, 48),
        name="gqa_attention",
    )(q, k, v)


CONV_HALO = 16
CONV_CHUNK = 64


def _conv_kernel(u_ref, w_ref, b_ref, g_ref, beta_ref, o_ref, pad_ref, sh_ref):
    seq = u_ref.shape[0]
    zeros = jnp.zeros((CONV_HALO, CONV_WIDTH), F32)
    pad_ref[0:CONV_HALO, :] = zeros
    pad_ref[CONV_HALO + seq:CONV_HALO + seq + CONV_HALO, :] = zeros
    pad_ref[CONV_HALO:CONV_HALO + seq, :] = u_ref[...].astype(F32)
    shift = CONV_HALO - CONV_KERNEL // 2
    span = CONV_CHUNK + (CONV_KERNEL // SUBLANES) * SUBLANES

    def chunk(c, carry):
        t0 = pl.multiple_of(c * CONV_CHUNK, CONV_CHUNK)
        win = pad_ref[pl.ds(t0, CONV_CHUNK + 2 * CONV_HALO), :]
        for res in range(1, SUBLANES):
            sh_ref[res - 1] = win[res:res + span]
        acc = jnp.zeros((CONV_CHUNK, CONV_WIDTH), F32)
        for tap in range(CONV_KERNEL):
            res = (tap + shift) % SUBLANES
            off = (tap + shift) - res
            if res == 0:
                rows = win[off:off + CONV_CHUNK]
            else:
                rows = sh_ref[res - 1, off:off + CONV_CHUNK, :]
            acc = acc + rows * w_ref[tap:tap + 1, :]
        acc = acc + b_ref[...]
        mu = jnp.mean(acc, axis=-1, keepdims=True)
        xc = acc - mu
        y = xc * lax.rsqrt(jnp.mean(xc * xc, axis=-1, keepdims=True) + EPS) * g_ref[...] + beta_ref[...]
        o_ref[pl.ds(t0, CONV_CHUNK), :] = (y * _sigmoid(y)).astype(BF16)
        return carry

    lax.fori_loop(0, seq // CONV_CHUNK, chunk, 0)


def _conv_branch(u, conv_w, conv_b, ln_g, ln_b, batch, seq):
    n = u.shape[0]
    const = lambda b: (0, 0)
    return pl.pallas_call(
        _conv_kernel,
        grid=(batch,),
        in_specs=[
            pl.BlockSpec((seq, CONV_WIDTH), lambda b: (b, 0)),
            pl.BlockSpec((CONV_KERNEL, CONV_WIDTH), const),
            pl.BlockSpec((1, CONV_WIDTH), const),
            pl.BlockSpec((1, CONV_WIDTH), const),
            pl.BlockSpec((1, CONV_WIDTH), const),
        ],
        out_specs=pl.BlockSpec((seq, CONV_WIDTH), lambda b: (b, 0)),
        out_shape=jax.ShapeDtypeStruct((n, CONV_WIDTH), BF16),
        scratch_shapes=[
            pltpu.VMEM((seq + 2 * CONV_HALO, CONV_WIDTH), F32),
            pltpu.VMEM((SUBLANES - 1, CONV_CHUNK + (CONV_KERNEL // SUBLANES) * SUBLANES, CONV_WIDTH), F32),
        ],
        compiler_params=_cparams(1, 32),
        name="conv_branch",
    )(u, conv_w, conv_b, ln_g, ln_b)


def _merge_kernel(xa_ref, xb_ref, a_ref, c_ref, gate_ref, wup_ref, wco_ref, wout_ref, gx_ref, wxq_ref,
                  x1_ref, qx_ref, *, tiles_a):
    attn = _dot(a_ref[...], wup_ref[...])
    conv = _dot(c_ref[...], wco_ref[...])
    merged = gate_ref[:, :D_MODEL].astype(F32) * attn + gate_ref[:, D_MODEL:].astype(F32) * conv
    x1 = _select_rows(xa_ref, xb_ref, tiles_a) + _dot(merged.astype(BF16), wout_ref[...])
    x1_ref[...] = x1
    h = _rms(x1, gx_ref[...]).astype(BF16)
    qx_ref[...] = (_dot(h, wxq_ref[...]) * (XHEAD_DIM ** -0.5)).astype(BF16)


def _merge(xa, xb, n, attn_o, conv_o, gates, w_up, w_co, w_out, g_cross, w_xq, tm):
    tiles_a = min(xa.shape[0], n) // tm
    row = lambda i: (i, 0)
    const = lambda i: (0, 0)
    return pl.pallas_call(
        functools.partial(_merge_kernel, tiles_a=tiles_a),
        grid=(n // tm,),
        in_specs=_split_specs(tm, tiles_a) + [
            pl.BlockSpec((tm, ATTN_WIDTH), row),
            pl.BlockSpec((tm, CONV_WIDTH), row),
            pl.BlockSpec((tm, 2 * D_MODEL), row),
            pl.BlockSpec((ATTN_WIDTH, D_MODEL), const),
            pl.BlockSpec((CONV_WIDTH, D_MODEL), const),
            pl.BlockSpec((D_MODEL, D_MODEL), const),
            pl.BlockSpec((1, D_MODEL), const),
            pl.BlockSpec((D_MODEL, D_MODEL), const),
        ],
        out_specs=[pl.BlockSpec((tm, D_MODEL), row), pl.BlockSpec((tm, D_MODEL), row)],
        out_shape=[jax.ShapeDtypeStruct((n, D_MODEL), F32), jax.ShapeDtypeStruct((n, D_MODEL), BF16)],
        compiler_params=_cparams(1, 48),
        name="merge_out_xq",
    )(xa, xb, attn_o, conv_o, gates, w_up, w_co, w_out, g_cross, w_xq)


def _memkv_kernel(m_ref, g_ref, w_ref, k_ref, v_ref):
    h = _rms(m_ref[...], g_ref[...]).astype(BF16)
    k_ref[...] = _dot(h, w_ref[:, :D_MODEL]).astype(BF16)
    v_ref[...] = _dot(h, w_ref[:, D_MODEL:]).astype(BF16)


def _mem_kv(mem, g_mem, w_xkv, tm):
    n = mem.shape[0]
    row = lambda i: (i, 0)
    const = lambda i: (0, 0)
    return pl.pallas_call(
        _memkv_kernel,
        grid=(n // tm,),
        in_specs=[
            pl.BlockSpec((tm, D_MODEL), row),
            pl.BlockSpec((1, D_MODEL), const),
            pl.BlockSpec((D_MODEL, 2 * D_MODEL), const),
        ],
        out_specs=[pl.BlockSpec((tm, D_MODEL), row), pl.BlockSpec((tm, D_MODEL), row)],
        out_shape=[jax.ShapeDtypeStruct((n, D_MODEL), BF16), jax.ShapeDtypeStruct((n, D_MODEL), BF16)],
        compiler_params=_cparams(1, 32),
        name="mem_kv",
    )(mem, g_mem, w_xkv)


def _route(lg, tri, base_ref):
    t = lg.shape[1]
    gl = lg[0:SUBLANES]
    gmax = jnp.max(gl, axis=0, keepdims=True)
    p_top = 1.0 / jnp.sum(jnp.exp(gl - gmax), axis=0, keepdims=True)
    iota_e = lax.broadcasted_iota(I32, (SUBLANES, t), 0).astype(F32)
    none = float(SUBLANES)
    grp = jnp.min(jnp.where(gl == gmax, iota_e, none), axis=0, keepdims=True)
    el = lg[SUBLANES:SUBLANES + N_EXPERTS]
    sel = el[0:EXPERTS_PER_GROUP]
    for g in range(1, N_GROUPS):
        sel = jnp.where(grp == float(g), el[g * EXPERTS_PER_GROUP:(g + 1) * EXPERTS_PER_GROUP], sel)
    v0 = jnp.max(sel, axis=0, keepdims=True)
    i0 = jnp.min(jnp.where(sel == v0, iota_e, none), axis=0, keepdims=True)
    rest = jnp.where(iota_e == i0, NEG_BIG, sel)
    v1 = jnp.max(rest, axis=0, keepdims=True)
    i1 = jnp.min(jnp.where(rest == v1, iota_e, none), axis=0, keepdims=True)
    ratio = jnp.exp(v1 - v0)
    g0 = p_top / (1.0 + ratio)
    g1 = p_top * ratio / (1.0 + ratio)
    e0 = grp * float(EXPERTS_PER_GROUP) + i0
    e1 = grp * float(EXPERTS_PER_GROUP) + i1
    iota_all = lax.broadcasted_iota(I32, (N_EXPERTS, t), 0).astype(F32)
    hit0 = iota_all == e0
    hit1 = iota_all == e1
    onehot = jnp.where(hit0 | hit1, 1.0, 0.0)
    before = _dot(onehot.astype(BF16), tri) + base_ref[:, 0:1]
    r0 = jnp.sum(jnp.where(hit0, before, 0.0), axis=0, keepdims=True)
    r1 = jnp.sum(jnp.where(hit1, before, 0.0), axis=0, keepdims=True)
    base_ref[...] = base_ref[...] + jnp.sum(onehot, axis=1, keepdims=True)
    return e0.astype(I32), e1.astype(I32), r0.astype(I32), r1.astype(I32), g0, g1


def _cross_kernel(x_ref, q_ref, k_ref, v_ref, wo_ref, gffn_ref, wrh_ref, wrl_ref, rb_ref, tri_ref,
                  x2_ref, hp_ref, idx_ref, gate_ref, cnt_ref, base_ref):
    @pl.when((pl.program_id(0) == 0) & (pl.program_id(1) == 0))
    def _():
        base_ref[...] = jnp.zeros_like(base_ref)

    heads = []
    for hd in range(N_XHEADS):
        sl = slice(hd * XHEAD_DIM, (hd + 1) * XHEAD_DIM)
        s = lax.dot_general(q_ref[:, sl], k_ref[:, sl], NT_DIMS, preferred_element_type=F32)
        m = jnp.max(s, axis=-1, keepdims=True)
        p = jnp.exp(s - m)
        l = jnp.sum(p, axis=-1, keepdims=True)
        heads.append(_dot(p.astype(BF16), v_ref[:, sl]) / l)
    o = jnp.concatenate(heads, axis=1).astype(BF16)
    x2 = x_ref[...] + _dot(o, wo_ref[...])
    x2_ref[...] = x2
    h = _rms(x2, gffn_ref[...])
    h_hi = h.astype(BF16)
    _store_token_rows(hp_ref, _pack_bf16_pairs(h))
    h_lo = (h - h_hi.astype(F32)).astype(BF16)
    lg = (lax.dot_general(wrh_ref[...], h_hi, NT_DIMS, preferred_element_type=F32)
          + lax.dot_general(wrh_ref[...], h_lo, NT_DIMS, preferred_element_type=F32)
          + lax.dot_general(wrl_ref[...], h_hi, NT_DIMS, preferred_element_type=F32)) + rb_ref[:, 0:1]
    e0, e1, r0, r1, g0, g1 = _route(lg, tri_ref[...], base_ref)
    t = lg.shape[1]
    idx_ref[...] = jnp.concatenate([e0, e1, r0, r1, jnp.zeros((SUBLANES - 4, t), I32)], axis=0)
    gl = jnp.concatenate([g0, g1, jnp.zeros((LANES - 2, t), F32)], axis=0)
    gate_ref[...] = gl.T
    cnt_ref[...] = base_ref[...]


def _cross_route(x1, qx, kx, vx, w_xo, g_ffn, wr_hi, wr_lo, r_bias, tri, batch, seq, mem_len, tq):
    n = x1.shape[0]
    nq = seq // tq
    row = lambda b, i: (b * nq + i, 0)
    per_b = lambda b, i: (b, 0)
    const = lambda b, i: (0, 0)
    return pl.pallas_call(
        _cross_kernel,
        grid=(batch, nq),
        in_specs=[
            pl.BlockSpec((tq, D_MODEL), row),
            pl.BlockSpec((tq, D_MODEL), row),
            pl.BlockSpec((mem_len, D_MODEL), per_b),
            pl.BlockSpec((mem_len, D_MODEL), per_b),
            pl.BlockSpec((D_MODEL, D_MODEL), const),
            pl.BlockSpec((1, D_MODEL), const),
            pl.BlockSpec((ROUTER_ROWS, D_MODEL), const),
            pl.BlockSpec((ROUTER_ROWS, D_MODEL), const),
            pl.BlockSpec((ROUTER_ROWS, LANES), const),
            pl.BlockSpec((tq, tq), const),
        ],
        out_specs=[
            pl.BlockSpec((tq, D_MODEL), row),
            pl.BlockSpec((tq * ROW_TILES, LANES), row),
            pl.BlockSpec((SUBLANES, tq), lambda b, i: (0, b * nq + i)),
            pl.BlockSpec((tq, LANES), row),
            pl.BlockSpec((N_EXPERTS, LANES), const),
        ],
        out_shape=[
            jax.ShapeDtypeStruct((n, D_MODEL), F32),
            jax.ShapeDtypeStruct((n * ROW_TILES, LANES), U32),
            jax.ShapeDtypeStruct((SUBLANES, n), I32),
            jax.ShapeDtypeStruct((n, LANES), F32),
            jax.ShapeDtypeStruct((N_EXPERTS, LANES), F32),
        ],
        scratch_shapes=[pltpu.VMEM((N_EXPERTS, LANES), F32)],
        compiler_params=_cparams(2, 48),
        name="cross_attn_router",
    )(x1, qx, kx, vx, w_xo, g_ffn, wr_hi, wr_lo, r_bias, tri)


DISPATCH_TOKENS = 512
ISSUE_UNROLL = 8


def _row_copy(src_ref, src_tok, dst_ref, dst_tok, sem):
    return pltpu.make_async_copy(src_ref.at[pl.ds(src_tok * ROW_TILES, ROW_TILES)],
                                 dst_ref.at[pl.ds(dst_tok * ROW_TILES, ROW_TILES)], sem)


def _load_slots(slots_hbm, step, slots_smem, sem):
    cp = pltpu.make_async_copy(slots_hbm.at[step], slots_smem, sem)
    cp.start()
    cp.wait()


def _dispatch_kernel(slots_hbm, hp_ref, xs_in, xs_out, slots_smem, idx_sem, row_sem):
    del xs_in
    _load_slots(slots_hbm, pl.program_id(0), slots_smem, idx_sem)

    def issue(group, c, carry):
        for k in range(2):
            slot = slots_smem[k * (SUBLANES // 2) + group, c]
            _row_copy(hp_ref, group * LANES + c, xs_out, slot, row_sem).start(priority=k)
        return carry

    for group in range(DISPATCH_TOKENS // LANES):
        lax.fori_loop(0, LANES, functools.partial(issue, group), 0, unroll=ISSUE_UNROLL)

    def drain(r, carry):
        _row_copy(hp_ref, 0, xs_out, 0, row_sem).wait()
        return carry

    lax.fori_loop(0, 2 * DISPATCH_TOKENS, drain, 0, unroll=64)


def _dispatch(slots, hp, xs_zero):
    n = hp.shape[0] // ROW_TILES
    return pl.pallas_call(
        _dispatch_kernel,
        grid=(n // DISPATCH_TOKENS,),
        in_specs=[
            pl.BlockSpec(memory_space=pl.ANY),
            pl.BlockSpec((DISPATCH_TOKENS * ROW_TILES, LANES), lambda i: (i, 0)),
            pl.BlockSpec(memory_space=pl.ANY),
        ],
        out_specs=pl.BlockSpec(memory_space=pl.ANY),
        out_shape=jax.ShapeDtypeStruct(xs_zero.shape, U32),
        scratch_shapes=[pltpu.SMEM((SUBLANES, LANES), I32), pltpu.SemaphoreType.DMA, pltpu.SemaphoreType.DMA],
        input_output_aliases={2: 0},
        compiler_params=_cparams(1, 16),
        name="moe_dispatch",
    )(slots, hp, xs_zero)


def _expert_kernel(be_ref, nused_ref, xs_ref, wg_ref, wu_ref, wd_ref, ys_ref):
    del be_ref
    live = pl.program_id(0) < nused_ref[0]

    @pl.when(live)
    def _():
        x = _unpack_bf16_pairs(_load_token_rows(xs_ref)).astype(BF16)
        gate = _dot(x, wg_ref[...])
        hid = (gate * _sigmoid(gate)) * _dot(x, wu_ref[...])
        _store_token_rows(ys_ref, _pack_bf16_pairs(_dot(hid.astype(BF16), wd_ref[...])))

    @pl.when(jnp.logical_not(live))
    def _():
        ys_ref[...] = jnp.zeros_like(ys_ref)


def _experts(block_e, n_used, xs, w_gate, w_up, w_down):
    n_blocks = xs.shape[0] // (EXPERT_ROWS * ROW_TILES)
    rows = lambda i, be, nu: (i, 0)
    by_expert = lambda i, be, nu: (be[i], 0, 0)
    grid_spec = pltpu.PrefetchScalarGridSpec(
        num_scalar_prefetch=2,
        grid=(n_blocks,),
        in_specs=[
            pl.BlockSpec((EXPERT_ROWS * ROW_TILES, LANES), rows),
            pl.BlockSpec((None, D_MODEL, D_FF_EXPERT), by_expert),
            pl.BlockSpec((None, D_MODEL, D_FF_EXPERT), by_expert),
            pl.BlockSpec((None, D_FF_EXPERT, D_MODEL), by_expert),
        ],
        out_specs=pl.BlockSpec((EXPERT_ROWS * ROW_TILES, LANES), rows),
    )
    return pl.pallas_call(
        _expert_kernel,
        grid_spec=grid_spec,
        out_shape=jax.ShapeDtypeStruct(xs.shape, U32),
        compiler_params=_cparams(1, 32),
        name="moe_experts",
    )(block_e, n_used, xs, w_gate, w_up, w_down)


def _combine_kernel(slots_hbm, ys_hbm, x_ref, gate_ref, gfin_ref, *out_and_scratch, tiles_a):
    if tiles_a is None:
        x3_ref, rows_ref, slots_smem, idx_sem, row_sem = out_and_scratch
    else:
        ya_ref, yb_ref, rows_ref, slots_smem, idx_sem, row_sem = out_and_scratch
    step = pl.program_id(0)
    _load_slots(slots_hbm, step, slots_smem, idx_sem)

    def issue(group, c, carry):
        for k in range(2):
            slot = slots_smem[k * (SUBLANES // 2) + group, c]
            _row_copy(ys_hbm, slot, rows_ref.at[k], group * LANES + c, row_sem).start(priority=k)
        return carry

    for group in range(DISPATCH_TOKENS // LANES):
        lax.fori_loop(0, LANES, functools.partial(issue, group), 0, unroll=ISSUE_UNROLL)

    def drain(r, carry):
        _row_copy(ys_hbm, 0, rows_ref.at[0], 0, row_sem).wait()
        return carry

    lax.fori_loop(0, 2 * DISPATCH_TOKENS, drain, 0, unroll=64)

    y0 = _unpack_bf16_pairs(_load_token_rows(rows_ref.at[0]))
    y1 = _unpack_bf16_pairs(_load_token_rows(rows_ref.at[1]))
    x3 = x_ref[...] + gate_ref[:, 0:1] * y0 + gate_ref[:, 1:2] * y1
    if tiles_a is None:
        x3_ref[...] = x3
    else:
        y = _rms(x3, gfin_ref[...])

        @pl.when(step < tiles_a)
        def _():
            ya_ref[...] = y

        @pl.when(step >= tiles_a)
        def _():
            yb_ref[...] = y


def _combine(slots, ys, x2, gates, g_final, rows_a):
    n = x2.shape[0]
    row = lambda i: (i, 0)
    const = lambda i: (0, 0)
    tok = DISPATCH_TOKENS
    if rows_a is None:
        tiles_a = None
        out_specs = [pl.BlockSpec((tok, D_MODEL), row)]
        out_shape = [jax.ShapeDtypeStruct((n, D_MODEL), F32)]
    else:
        tiles_a = rows_a // tok
        out_specs = [pl.BlockSpec((tok, D_MODEL), lambda i: (jnp.minimum(i, tiles_a - 1), 0)),
                     pl.BlockSpec((tok, D_MODEL), lambda i: (jnp.maximum(i - tiles_a, 0), 0))]
        out_shape = [jax.ShapeDtypeStruct((rows_a, D_MODEL), F32), jax.ShapeDtypeStruct((n - rows_a, D_MODEL), F32)]
    out = pl.pallas_call(
        functools.partial(_combine_kernel, tiles_a=tiles_a),
        grid=(n // tok,),
        in_specs=[
            pl.BlockSpec(memory_space=pl.ANY),
            pl.BlockSpec(memory_space=pl.ANY),
            pl.BlockSpec((tok, D_MODEL), row),
            pl.BlockSpec((tok, LANES), row),
            pl.BlockSpec((1, D_MODEL), const),
        ],
        out_specs=out_specs,
        out_shape=out_shape,
        scratch_shapes=[
            pltpu.VMEM((2, tok * ROW_TILES, LANES), U32),
            pltpu.SMEM((SUBLANES, LANES), I32),
            pltpu.SemaphoreType.DMA,
            pltpu.SemaphoreType.DMA,
        ],
        compiler_params=_cparams(1, 32),
        name="moe_combine" if rows_a is None else "moe_combine_final",
    )(slots, ys, x2, gates, g_final)
    return out


def _rope_tables(seq):
    rows = seq // GRID_W
    row_idx = jnp.repeat(jnp.arange(rows, dtype=F32), GRID_W)
    col_idx = jnp.tile(jnp.arange(GRID_W, dtype=F32), rows)
    n_freq = HEAD_DIM // 4
    inv_freq = ROPE_THETA ** (-jnp.arange(n_freq, dtype=F32) / n_freq)
    ang_row = row_idx[:, None] * inv_freq
    ang_col = col_idx[:, None] * inv_freq
    cos = jnp.concatenate([jnp.cos(ang_row)] * 2 + [jnp.cos(ang_col)] * 2, axis=1)
    sin = jnp.concatenate([-jnp.sin(ang_row), jnp.sin(ang_row), -jnp.sin(ang_col), jnp.sin(ang_col)], axis=1)
    return jnp.tile(cos, (1, LANES // HEAD_DIM)), jnp.tile(sin, (1, LANES // HEAD_DIM))


def _segment_ones():
    head = np.arange(ATTN_WIDTH) // HEAD_DIM
    return jnp.asarray(head[:, None] == head[None, :], BF16)


def _router_weights(w_group, b_group, w_router, b_router):
    w = jnp.zeros((ROUTER_ROWS, D_MODEL), F32)
    w = w.at[0:N_GROUPS].set(w_group.T).at[SUBLANES:SUBLANES + N_EXPERTS].set(w_router.T)
    hi = w.astype(BF16)
    lo = (w - hi.astype(F32)).astype(BF16)
    b = jnp.full((ROUTER_ROWS,), NEG_BIG, F32).at[0:N_GROUPS].set(b_group)
    b = b.at[SUBLANES:SUBLANES + N_EXPERTS].set(b_router)
    return hi, lo, jnp.tile(b[:, None], (1, LANES))


def _slot_plan(idx, counts, n):
    cnt = counts[:, 0].astype(I32)
    padded = (cnt + EXPERT_ROWS - 1) // EXPERT_ROWS * EXPERT_ROWS
    pad_end = jnp.cumsum(padded)
    pad_start = pad_end - padded
    experts = jnp.arange(N_EXPERTS, dtype=I32)
    start_of = lambda e: jnp.sum(jnp.where(e[:, None] == experts[None, :], pad_start[None, :], 0), axis=1)
    slot0 = start_of(idx[0]) + idx[2]
    slot1 = start_of(idx[1]) + idx[3]
    n_tiles = n // DISPATCH_TOKENS
    per_k = SUBLANES // 2
    slots = jnp.stack([slot0, slot1]).reshape(2, n_tiles, per_k, LANES)
    slots = slots.transpose(1, 0, 2, 3).reshape(n_tiles, SUBLANES, LANES)
    n_blocks = (2 * n) // EXPERT_ROWS + N_EXPERTS
    block_first_row = jnp.arange(n_blocks, dtype=I32) * EXPERT_ROWS
    ends_passed = jnp.sum((pad_end[None, :] <= block_first_row[:, None]).astype(I32), axis=1)
    block_e = jnp.minimum(ends_passed, N_EXPERTS - 1)
    n_used = (pad_end[-1:] // EXPERT_ROWS).astype(I32)
    return slots, block_e, n_used, n_blocks


def kernel(x_prompt, x_sample, mem_prompt, mem_sample, g_mix, w_in, g_q, g_k, b_gate, w_attn_up, conv_w, conv_b, ln_conv_g, ln_conv_b, w_conv_out, w_out, g_cross, g_mem, w_xq, w_xkv, w_xo, g_ffn, w_group, b_group, w_router, b_router, w_e_gate, w_e_up, w_e_down, g_final):
    bp, seq, d = x_prompt.shape
    bs = x_sample.shape[0]
    assert x_sample.shape[1] == seq and d == D_MODEL and seq % GRID_W == 0
    batch = bp + bs
    mem_len = mem_prompt.shape[1]
    n = batch * seq
    depth = g_mix.shape[0]
    tm = min(512, seq)
    tq_attn = min(512, seq)
    assert seq % tm == 0 and n % DISPATCH_TOKENS == 0 and (2 * n) % EXPERT_ROWS == 0 and seq % CONV_CHUNK == 0
    assert (bp * seq) % DISPATCH_TOKENS == 0

    xa = x_prompt.reshape(bp * seq, d)
    xb = x_sample.reshape(bs * seq, d)
    mem = jnp.concatenate([mem_prompt.reshape(bp * mem_len, d), mem_sample.reshape(bs * mem_len, d)], axis=0)
    cos_t, sin_t = _rope_tables(seq)
    seg = _segment_ones()
    tri = jnp.asarray(np.triu(np.ones((tm, tm), np.float32), k=1), BF16)
    row2 = lambda a: a.reshape(1, -1).astype(F32)

    out = None
    for l in range(depth):
        q, k, v, u, gates = _inproj(
            xa, xb, n, row2(g_mix[l]), w_in[l].astype(BF16), row2(jnp.tile(g_q[l], N_HEADS)),
            row2(jnp.tile(g_k[l], N_KV_HEADS)), cos_t, sin_t, row2(b_gate[l]), seg, seq, tm)
        attn_o = _attention(q, k, v, batch, seq, tq_attn)
        conv_o = _conv_branch(u, conv_w[l], row2(conv_b[l]), row2(ln_conv_g[l]), row2(ln_conv_b[l]), batch, seq)
        x1, qx = _merge(xa, xb, n, attn_o, conv_o, gates, w_attn_up[l].astype(BF16), w_conv_out[l].astype(BF16),
                        w_out[l].astype(BF16), row2(g_cross[l]), w_xq[l].astype(BF16), tm)
        kx, vx = _mem_kv(mem, row2(g_mem[l]), w_xkv[l].astype(BF16), min(512, mem.shape[0]))
        wr_hi, wr_lo, r_bias = _router_weights(w_group[l], b_group[l], w_router[l], b_router[l])
        x2, hp, idx, gate_tm, counts = _cross_route(
            x1, qx, kx, vx, w_xo[l].astype(BF16), row2(g_ffn[l]), wr_hi, wr_lo, r_bias, tri,
            batch, seq, mem_len, tm)
        slots, block_e, n_used, n_blocks = _slot_plan(idx, counts, n)
        xs = _dispatch(slots, hp, jnp.zeros((n_blocks * EXPERT_ROWS * ROW_TILES, LANES), U32))
        ys = _experts(block_e, n_used, xs, w_e_gate[l].astype(BF16), w_e_up[l].astype(BF16),
                      w_e_down[l].astype(BF16))
        last = l == depth - 1
        out = _combine(slots, ys, x2, gate_tm, row2(g_final), bp * seq if last else None)
        xa = xb = out[0]
    return (out[0].reshape(bp, seq, d), out[1].reshape(bs, seq, d))
```

```python
import functools

import numpy as np
import jax
import jax.numpy as jnp
from jax import lax
from jax.experimental import pallas as pl
from jax.experimental.pallas import tpu as pltpu

F32 = jnp.float32
BF16 = jnp.bfloat16
U32 = jnp.uint32
I32 = jnp.int32

EPS = 1e-6
D_MODEL = 1024
GRID_W = 64
N_HEADS = 8
N_KV_HEADS = 2
HEAD_DIM = 64
ROPE_THETA = 10000.0
ATTN_WIDTH = N_HEADS * HEAD_DIM
KV_WIDTH = N_KV_HEADS * HEAD_DIM
CONV_WIDTH = 512
CONV_KERNEL = 31
N_XHEADS = 4
XHEAD_DIM = D_MODEL // N_XHEADS
N_GROUPS = 4
EXPERTS_PER_GROUP = 8
N_EXPERTS = N_GROUPS * EXPERTS_PER_GROUP
D_FF_EXPERT = 512

V7X_VMEM_BYTES = 64 * 1024 * 1024
LANES = 128
SUBLANES = 8
EXPERT_ROWS = 256
ROUTER_ROWS = 40
PACKED = D_MODEL // 2
NEG_BIG = float(np.finfo(np.float32).min)
NT_DIMS = (((1,), (1,)), ((), ()))


def _cparams(n_axes, vmem_mib):
    return pltpu.CompilerParams(
        dimension_semantics=("arbitrary",) * n_axes,
        vmem_limit_bytes=min(vmem_mib * 1024 * 1024, V7X_VMEM_BYTES - 8 * 1024 * 1024))


def _dot(a, b):
    return jnp.dot(a, b, preferred_element_type=F32)


def _sigmoid(x):
    return 1.0 / (1.0 + jnp.exp(-x))


def _rms(x, g):
    return x * lax.rsqrt(jnp.mean(x * x, axis=-1, keepdims=True) + EPS) * g


def _pack_bf16_pairs(y):
    w = y.shape[1] // 2
    yb = y.astype(BF16).astype(F32)
    hi = lax.bitcast_convert_type(yb[:, :w], U32)
    lo = lax.bitcast_convert_type(yb[:, w:], U32)
    return (hi & jnp.uint32(0xFFFF0000)) | (lo >> 16)


def _unpack_bf16_pairs(p):
    hi = lax.bitcast_convert_type(p & jnp.uint32(0xFFFF0000), F32)
    lo = lax.bitcast_convert_type(p << 16, F32)
    return jnp.concatenate([hi, lo], axis=1)


ROW_TILES = PACKED // LANES


def _store_token_rows(ref, packed):
    t = packed.shape[0]
    for j in range(ROW_TILES):
        ref[pl.ds(j, t, stride=ROW_TILES), :] = packed[:, j * LANES:(j + 1) * LANES]


def _load_token_rows(ref):
    t = ref.shape[0] // ROW_TILES
    return jnp.concatenate([ref[pl.ds(j, t, stride=ROW_TILES), :] for j in range(ROW_TILES)], axis=1)


def _head_norm_rope(raw, gain, seg, cos, sin, scale):
    width = raw.shape[1]
    sq = raw * raw
    ss = _dot(sq.astype(BF16), seg)
    qn = raw * lax.rsqrt(ss * (1.0 / HEAD_DIM) + EPS) * gain
    reps = width // LANES
    cos_w = jnp.concatenate([cos] * reps, axis=1) if reps > 1 else cos
    sin_w = jnp.concatenate([sin] * reps, axis=1) if reps > 1 else sin
    lane = lax.broadcasted_iota(I32, qn.shape, 1)
    first_half = (lane & 16) == 0
    partner = jnp.where(first_half, pltpu.roll(qn, width - 16, 1), pltpu.roll(qn, 16, 1))
    return (qn * cos_w + partner * sin_w) * scale


def _select_rows(xa_ref, xb_ref, tiles_a):
    return jnp.where(pl.program_id(0) < tiles_a, xa_ref[...], xb_ref[...])


def _split_specs(tm, tiles_a):
    return [pl.BlockSpec((tm, D_MODEL), lambda i: (jnp.minimum(i, tiles_a - 1), 0)),
            pl.BlockSpec((tm, D_MODEL), lambda i: (jnp.maximum(i - tiles_a, 0), 0))]


def _inproj_kernel(xa_ref, xb_ref, gmix_ref, w_ref, gq_ref, gk_ref, cos_ref, sin_ref, bg_ref, seg_ref,
                   q_ref, k_ref, v_ref, u_ref, gate_ref, *, tiles_a):
    h = _rms(_select_rows(xa_ref, xb_ref, tiles_a), gmix_ref[...]).astype(BF16)
    cos = cos_ref[...]
    sin = sin_ref[...]
    c0 = 0
    q_raw = _dot(h, w_ref[:, c0:c0 + ATTN_WIDTH])
    q_ref[...] = _head_norm_rope(q_raw, gq_ref[...], seg_ref[...], cos, sin, HEAD_DIM ** -0.5).astype(BF16)
    c0 += ATTN_WIDTH
    k_raw = _dot(h, w_ref[:, c0:c0 + KV_WIDTH])
    k_ref[...] = _head_norm_rope(k_raw, gk_ref[...], seg_ref[:KV_WIDTH, :KV_WIDTH], cos, sin, 1.0).astype(BF16)
    c0 += KV_WIDTH
    v_ref[...] = _dot(h, w_ref[:, c0:c0 + KV_WIDTH]).astype(BF16)
    c0 += KV_WIDTH
    c_a = _dot(h, w_ref[:, c0:c0 + CONV_WIDTH])
    c0 += CONV_WIDTH
    c_b = _dot(h, w_ref[:, c0:c0 + CONV_WIDTH])
    c0 += CONV_WIDTH
    u_ref[...] = (c_a * _sigmoid(c_b)).astype(BF16)
    chunk = 512
    for j in range(2 * D_MODEL // chunk):
        lg = _dot(h, w_ref[:, c0 + j * chunk:c0 + (j + 1) * chunk]) + bg_ref[:, j * chunk:(j + 1) * chunk]
        gate_ref[:, j * chunk:(j + 1) * chunk] = _sigmoid(lg).astype(BF16)


def _inproj(xa, xb, n, gmix, w_in, gq_t, gk_t, cos_t, sin_t, b_gate, seg, seq, tm):
    tiles_a = min(xa.shape[0], n) // tm
    in_cols = w_in.shape[1]
    n_seq_tiles = seq // tm
    row = lambda i: (i, 0)
    const = lambda i: (0, 0)
    pos = lambda i: (i % n_seq_tiles, 0)
    return pl.pallas_call(
        functools.partial(_inproj_kernel, tiles_a=tiles_a),
        grid=(n // tm,),
        in_specs=_split_specs(tm, tiles_a) + [
            pl.BlockSpec((1, D_MODEL), const),
            pl.BlockSpec((D_MODEL, in_cols), const),
            pl.BlockSpec((1, ATTN_WIDTH), const),
            pl.BlockSpec((1, KV_WIDTH), const),
            pl.BlockSpec((tm, LANES), pos),
            pl.BlockSpec((tm, LANES), pos),
            pl.BlockSpec((1, 2 * D_MODEL), const),
            pl.BlockSpec((ATTN_WIDTH, ATTN_WIDTH), const),
        ],
        out_specs=[
            pl.BlockSpec((tm, ATTN_WIDTH), row),
            pl.BlockSpec((tm, KV_WIDTH), row),
            pl.BlockSpec((tm, KV_WIDTH), row),
            pl.BlockSpec((tm, CONV_WIDTH), row),
            pl.BlockSpec((tm, 2 * D_MODEL), row),
        ],
        out_shape=[
            jax.ShapeDtypeStruct((n, ATTN_WIDTH), BF16),
            jax.ShapeDtypeStruct((n, KV_WIDTH), BF16),
            jax.ShapeDtypeStruct((n, KV_WIDTH), BF16),
            jax.ShapeDtypeStruct((n, CONV_WIDTH), BF16),
            jax.ShapeDtypeStruct((n, 2 * D_MODEL), BF16),
        ],
        compiler_params=_cparams(1, 48),
        name="inproj",
    )(xa, xb, gmix, w_in, gq_t, gk_t, cos_t, sin_t, b_gate, seg)


def _attn_kernel(q_ref, k_ref, v_ref, o_ref):
    group = N_HEADS // N_KV_HEADS
    outs = []
    for hd in range(N_HEADS):
        g = hd // group
        kg = k_ref[:, g * HEAD_DIM:(g + 1) * HEAD_DIM]
        vg = v_ref[:, g * HEAD_DIM:(g + 1) * HEAD_DIM]
        s = lax.dot_general(q_ref[:, hd * HEAD_DIM:(hd + 1) * HEAD_DIM], kg, NT_DIMS, preferred_element_type=F32)
        m = jnp.max(s, axis=-1, keepdims=True)
        p = jnp.exp(s - m)
        l = jnp.sum(p, axis=-1, keepdims=True)
        outs.append(_dot(p.astype(BF16), vg) / l)
    o_ref[...] = jnp.concatenate(outs, axis=1).astype(BF16)


def _attention(q, k, v, batch, seq, tq):
    n = q.shape[0]
    nq = seq // tq
    return pl.pallas_call(
        _attn_kernel,
        grid=(batch, nq),
        in_specs=[
            pl.BlockSpec((tq, ATTN_WIDTH), lambda b, i: (b * nq + i, 0)),
            pl.BlockSpec((seq, KV_WIDTH), lambda b, i: (b, 0)),
            pl.BlockSpec((seq, KV_WIDTH), lambda b, i: (b, 0)),
        ],
        out_specs=pl.BlockSpec((tq, ATTN_WIDTH), lambda b, i: (b * nq + i, 0)),
        out_shape=jax.ShapeDtypeStruct((n, ATTN_WIDTH), BF16),
        compiler_params=_cparams(2, 48),
        name="gqa_attention",
    )(q, k, v)


CONV_HALO = 16
CONV_CHUNK = 64
CONV_WINDOW = CONV_CHUNK + 2 * CONV_HALO
CONV_SPAN = CONV_CHUNK + (CONV_KERNEL // SUBLANES) * SUBLANES


def _conv_shift_matrix():
    m = np.zeros((SUBLANES * CONV_SPAN, CONV_WINDOW), np.float32)
    for res in range(SUBLANES):
        rows = np.arange(CONV_SPAN)
        m[res * CONV_SPAN + rows, rows + res] = 1.0
    return jnp.asarray(m, BF16)


def _conv_kernel(u_ref, shift_ref, w_ref, b_ref, g_ref, beta_ref, o_ref, pad_ref):
    seq = u_ref.shape[0]
    zeros = jnp.zeros((CONV_HALO, CONV_WIDTH), BF16)
    pad_ref[0:CONV_HALO, :] = zeros
    pad_ref[CONV_HALO + seq:CONV_HALO + seq + CONV_HALO, :] = zeros
    pad_ref[CONV_HALO:CONV_HALO + seq, :] = u_ref[...]
    shift = CONV_HALO - CONV_KERNEL // 2

    def chunk(c, carry):
        t0 = pl.multiple_of(c * CONV_CHUNK, CONV_CHUNK)
        win = pad_ref[pl.ds(t0, CONV_WINDOW), :]
        acc = jnp.zeros((CONV_CHUNK, CONV_WIDTH), F32)
        for res in range(SUBLANES):
            shifted = _dot(shift_ref[res * CONV_SPAN:(res + 1) * CONV_SPAN, :], win)
            for tap in range(CONV_KERNEL):
                if (tap + shift) % SUBLANES == res:
                    r0 = (tap + shift) - res
                    acc = acc + shifted[r0:r0 + CONV_CHUNK] * w_ref[tap:tap + 1, :]
        acc = acc + b_ref[...]
        mu = jnp.mean(acc, axis=-1, keepdims=True)
        xc = acc - mu
        y = xc * lax.rsqrt(jnp.mean(xc * xc, axis=-1, keepdims=True) + EPS) * g_ref[...] + beta_ref[...]
        o_ref[pl.ds(t0, CONV_CHUNK), :] = (y * _sigmoid(y)).astype(BF16)
        return carry

    lax.fori_loop(0, seq // CONV_CHUNK, chunk, 0, unroll=2)


def _conv_branch(u, conv_w, conv_b, ln_g, ln_b, batch, seq):
    n = u.shape[0]
    const = lambda b: (0, 0)
    return pl.pallas_call(
        _conv_kernel,
        grid=(batch,),
        in_specs=[
            pl.BlockSpec((seq, CONV_WIDTH), lambda b: (b, 0)),
            pl.BlockSpec((SUBLANES * CONV_SPAN, CONV_WINDOW), const),
            pl.BlockSpec((CONV_KERNEL, CONV_WIDTH), const),
            pl.BlockSpec((1, CONV_WIDTH), const),
            pl.BlockSpec((1, CONV_WIDTH), const),
            pl.BlockSpec((1, CONV_WIDTH), const),
        ],
        out_specs=pl.BlockSpec((seq, CONV_WIDTH), lambda b: (b, 0)),
        out_shape=jax.ShapeDtypeStruct((n, CONV_WIDTH), BF16),
        scratch_shapes=[pltpu.VMEM((seq + 2 * CONV_HALO, CONV_WIDTH), BF16)],
        compiler_params=_cparams(1, 32),
        name="conv_branch",
    )(u, _conv_shift_matrix(), conv_w, conv_b, ln_g, ln_b)


def _merge_kernel(xa_ref, xb_ref, a_ref, c_ref, gate_ref, wup_ref, wco_ref, wout_ref, gx_ref, wxq_ref,
                  x1_ref, qx_ref, *, tiles_a):
    attn = _dot(a_ref[...], wup_ref[...])
    conv = _dot(c_ref[...], wco_ref[...])
    merged = gate_ref[:, :D_MODEL].astype(F32) * attn + gate_ref[:, D_MODEL:].astype(F32) * conv
    x1 = _select_rows(xa_ref, xb_ref, tiles_a) + _dot(merged.astype(BF16), wout_ref[...])
    x1_ref[...] = x1
    h = _rms(x1, gx_ref[...]).astype(BF16)
    qx_ref[...] = (_dot(h, wxq_ref[...]) * (XHEAD_DIM ** -0.5)).astype(BF16)


def _merge(xa, xb, n, attn_o, conv_o, gates, w_up, w_co, w_out, g_cross, w_xq, tm):
    tiles_a = min(xa.shape[0], n) // tm
    row = lambda i: (i, 0)
    const = lambda i: (0, 0)
    return pl.pallas_call(
        functools.partial(_merge_kernel, tiles_a=tiles_a),
        grid=(n // tm,),
        in_specs=_split_specs(tm, tiles_a) + [
            pl.BlockSpec((tm, ATTN_WIDTH), row),
            pl.BlockSpec((tm, CONV_WIDTH), row),
            pl.BlockSpec((tm, 2 * D_MODEL), row),
            pl.BlockSpec((ATTN_WIDTH, D_MODEL), const),
            pl.BlockSpec((CONV_WIDTH, D_MODEL), const),
            pl.BlockSpec((D_MODEL, D_MODEL), const),
            pl.BlockSpec((1, D_MODEL), const),
            pl.BlockSpec((D_MODEL, D_MODEL), const),
        ],
        out_specs=[pl.BlockSpec((tm, D_MODEL), row), pl.BlockSpec((tm, D_MODEL), row)],
        out_shape=[jax.ShapeDtypeStruct((n, D_MODEL), F32), jax.ShapeDtypeStruct((n, D_MODEL), BF16)],
        compiler_params=_cparams(1, 48),
        name="merge_out_xq",
    )(xa, xb, attn_o, conv_o, gates, w_up, w_co, w_out, g_cross, w_xq)


def _memkv_kernel(m_ref, g_ref, w_ref, k_ref, v_ref):
    h = _rms(m_ref[...], g_ref[...]).astype(BF16)
    k_ref[...] = _dot(h, w_ref[:, :D_MODEL]).astype(BF16)
    v_ref[...] = _dot(h, w_ref[:, D_MODEL:]).astype(BF16)


def _mem_kv(mem, g_mem, w_xkv, tm):
    n = mem.shape[0]
    row = lambda i: (i, 0)
    const = lambda i: (0, 0)
    return pl.pallas_call(
        _memkv_kernel,
        grid=(n // tm,),
        in_specs=[
            pl.BlockSpec((tm, D_MODEL), row),
            pl.BlockSpec((1, D_MODEL), const),
            pl.BlockSpec((D_MODEL, 2 * D_MODEL), const),
        ],
        out_specs=[pl.BlockSpec((tm, D_MODEL), row), pl.BlockSpec((tm, D_MODEL), row)],
        out_shape=[jax.ShapeDtypeStruct((n, D_MODEL), BF16), jax.ShapeDtypeStruct((n, D_MODEL), BF16)],
        compiler_params=_cparams(1, 32),
        name="mem_kv",
    )(mem, g_mem, w_xkv)


def _route(lg, tri, base_ref):
    t = lg.shape[1]
    gl = lg[0:SUBLANES]
    gmax = jnp.max(gl, axis=0, keepdims=True)
    p_top = 1.0 / jnp.sum(jnp.exp(gl - gmax), axis=0, keepdims=True)
    iota_e = lax.broadcasted_iota(I32, (SUBLANES, t), 0).astype(F32)
    none = float(SUBLANES)
    grp = jnp.min(jnp.where(gl == gmax, iota_e, none), axis=0, keepdims=True)
    el = lg[SUBLANES:SUBLANES + N_EXPERTS]
    sel = el[0:EXPERTS_PER_GROUP]
    for g in range(1, N_GROUPS):
        sel = jnp.where(grp == float(g), el[g * EXPERTS_PER_GROUP:(g + 1) * EXPERTS_PER_GROUP], sel)
    v0 = jnp.max(sel, axis=0, keepdims=True)
    i0 = jnp.min(jnp.where(sel == v0, iota_e, none), axis=0, keepdims=True)
    rest = jnp.where(iota_e == i0, NEG_BIG, sel)
    v1 = jnp.max(rest, axis=0, keepdims=True)
    i1 = jnp.min(jnp.where(rest == v1, iota_e, none), axis=0, keepdims=True)
    ratio = jnp.exp(v1 - v0)
    g0 = p_top / (1.0 + ratio)
    g1 = p_top * ratio / (1.0 + ratio)
    e0 = grp * float(EXPERTS_PER_GROUP) + i0
    e1 = grp * float(EXPERTS_PER_GROUP) + i1
    iota_all = lax.broadcasted_iota(I32, (N_EXPERTS, t), 0).astype(F32)
    hit0 = iota_all == e0
    hit1 = iota_all == e1
    onehot = jnp.where(hit0 | hit1, 1.0, 0.0)
    before = _dot(onehot.astype(BF16), tri) + base_ref[:, 0:1]
    r0 = jnp.sum(jnp.where(hit0, before, 0.0), axis=0, keepdims=True)
    r1 = jnp.sum(jnp.where(hit1, before, 0.0), axis=0, keepdims=True)
    base_ref[...] = base_ref[...] + jnp.sum(onehot, axis=1, keepdims=True)
    return e0.astype(I32), e1.astype(I32), r0.astype(I32), r1.astype(I32), g0, g1


def _cross_kernel(x_ref, q_ref, k_ref, v_ref, wo_ref, gffn_ref, wrh_ref, wrl_ref, rb_ref, tri_ref,
                  x2_ref, hp_ref, idx_ref, gate_ref, cnt_ref, base_ref):
    @pl.when((pl.program_id(0) == 0) & (pl.program_id(1) == 0))
    def _():
        base_ref[...] = jnp.zeros_like(base_ref)

    heads = []
    for hd in range(N_XHEADS):
        sl = slice(hd * XHEAD_DIM, (hd + 1) * XHEAD_DIM)
        s = lax.dot_general(q_ref[:, sl], k_ref[:, sl], NT_DIMS, preferred_element_type=F32)
        m = jnp.max(s, axis=-1, keepdims=True)
        p = jnp.exp(s - m)
        l = jnp.sum(p, axis=-1, keepdims=True)
        heads.append(_dot(p.astype(BF16), v_ref[:, sl]) / l)
    o = jnp.concatenate(heads, axis=1).astype(BF16)
    x2 = x_ref[...] + _dot(o, wo_ref[...])
    x2_ref[...] = x2
    h = _rms(x2, gffn_ref[...])
    h_hi = h.astype(BF16)
    _store_token_rows(hp_ref, _pack_bf16_pairs(h))
    h_lo = (h - h_hi.astype(F32)).astype(BF16)
    lg = (lax.dot_general(wrh_ref[...], h_hi, NT_DIMS, preferred_element_type=F32)
          + lax.dot_general(wrh_ref[...], h_lo, NT_DIMS, preferred_element_type=F32)
          + lax.dot_general(wrl_ref[...], h_hi, NT_DIMS, preferred_element_type=F32)) + rb_ref[:, 0:1]
    e0, e1, r0, r1, g0, g1 = _route(lg, tri_ref[...], base_ref)
    t = lg.shape[1]
    idx_ref[...] = jnp.concatenate([e0, e1, r0, r1, jnp.zeros((SUBLANES - 4, t), I32)], axis=0)
    gl = jnp.concatenate([g0, g1, jnp.zeros((LANES - 2, t), F32)], axis=0)
    gate_ref[...] = gl.T
    cnt_ref[...] = base_ref[...]


def _cross_route(x1, qx, kx, vx, w_xo, g_ffn, wr_hi, wr_lo, r_bias, tri, batch, seq, mem_len, tq):
    n = x1.shape[0]
    nq = seq // tq
    row = lambda b, i: (b * nq + i, 0)
    per_b = lambda b, i: (b, 0)
    const = lambda b, i: (0, 0)
    return pl.pallas_call(
        _cross_kernel,
        grid=(batch, nq),
        in_specs=[
            pl.BlockSpec((tq, D_MODEL), row),
            pl.BlockSpec((tq, D_MODEL), row),
            pl.BlockSpec((mem_len, D_MODEL), per_b),
            pl.BlockSpec((mem_len, D_MODEL), per_b),
            pl.BlockSpec((D_MODEL, D_MODEL), const),
            pl.BlockSpec((1, D_MODEL), const),
            pl.BlockSpec((ROUTER_ROWS, D_MODEL), const),
            pl.BlockSpec((ROUTER_ROWS, D_MODEL), const),
            pl.BlockSpec((ROUTER_ROWS, LANES), const),
            pl.BlockSpec((tq, tq), const),
        ],
        out_specs=[
            pl.BlockSpec((tq, D_MODEL), row),
            pl.BlockSpec((tq * ROW_TILES, LANES), row),
            pl.BlockSpec((SUBLANES, tq), lambda b, i: (0, b * nq + i)),
            pl.BlockSpec((tq, LANES), row),
            pl.BlockSpec((N_EXPERTS, LANES), const),
        ],
        out_shape=[
            jax.ShapeDtypeStruct((n, D_MODEL), F32),
            jax.ShapeDtypeStruct((n * ROW_TILES, LANES), U32),
            jax.ShapeDtypeStruct((SUBLANES, n), I32),
            jax.ShapeDtypeStruct((n, LANES), F32),
            jax.ShapeDtypeStruct((N_EXPERTS, LANES), F32),
        ],
        scratch_shapes=[pltpu.VMEM((N_EXPERTS, LANES), F32)],
        compiler_params=_cparams(2, 48),
        name="cross_attn_router",
    )(x1, qx, kx, vx, w_xo, g_ffn, wr_hi, wr_lo, r_bias, tri)


DISPATCH_TOKENS = 512
ISSUE_UNROLL = 8


def _row_copy(src_ref, src_tok, dst_ref, dst_tok, sem):
    return pltpu.make_async_copy(src_ref.at[pl.ds(src_tok * ROW_TILES, ROW_TILES)],
                                 dst_ref.at[pl.ds(dst_tok * ROW_TILES, ROW_TILES)], sem)


def _load_slots(slots_hbm, step, slots_smem, sem):
    cp = pltpu.make_async_copy(slots_hbm.at[step], slots_smem, sem)
    cp.start()
    cp.wait()


def _dispatch_kernel(slots_hbm, hp_ref, xs_in, xs_out, slots_smem, idx_sem, row_sem):
    del xs_in
    _load_slots(slots_hbm, pl.program_id(0), slots_smem, idx_sem)

    def issue(group, c, carry):
        for k in range(2):
            slot = slots_smem[k * (SUBLANES // 2) + group, c]
            _row_copy(hp_ref, group * LANES + c, xs_out, slot, row_sem).start(priority=k)
        return carry

    for group in range(DISPATCH_TOKENS // LANES):
        lax.fori_loop(0, LANES, functools.partial(issue, group), 0, unroll=ISSUE_UNROLL)

    def drain(r, carry):
        _row_copy(hp_ref, 0, xs_out, 0, row_sem).wait()
        return carry

    lax.fori_loop(0, 2 * DISPATCH_TOKENS, drain, 0, unroll=64)


def _dispatch(slots, hp, xs_zero):
    n = hp.shape[0] // ROW_TILES
    return pl.pallas_call(
        _dispatch_kernel,
        grid=(n // DISPATCH_TOKENS,),
        in_specs=[
            pl.BlockSpec(memory_space=pl.ANY),
            pl.BlockSpec((DISPATCH_TOKENS * ROW_TILES, LANES), lambda i: (i, 0)),
            pl.BlockSpec(memory_space=pl.ANY),
        ],
        out_specs=pl.BlockSpec(memory_space=pl.ANY),
        out_shape=jax.ShapeDtypeStruct(xs_zero.shape, U32),
        scratch_shapes=[pltpu.SMEM((SUBLANES, LANES), I32), pltpu.SemaphoreType.DMA, pltpu.SemaphoreType.DMA],
        input_output_aliases={2: 0},
        compiler_params=_cparams(1, 16),
        name="moe_dispatch",
    )(slots, hp, xs_zero)


def _expert_kernel(be_ref, nused_ref, xs_ref, wg_ref, wu_ref, wd_ref, ys_ref, wgb_ref, wub_ref, wdb_ref):
    step = pl.program_id(0)
    live = step < nused_ref[0]

    @pl.when((step == 0) | (be_ref[step] != be_ref[jnp.maximum(step - 1, 0)]))
    def _():
        wgb_ref[...] = wg_ref[...].astype(BF16)
        wub_ref[...] = wu_ref[...].astype(BF16)
        wdb_ref[...] = wd_ref[...].astype(BF16)

    @pl.when(live)
    def _():
        x = _unpack_bf16_pairs(_load_token_rows(xs_ref)).astype(BF16)
        gate = _dot(x, wgb_ref[...])
        hid = (gate * _sigmoid(gate)) * _dot(x, wub_ref[...])
        _store_token_rows(ys_ref, _pack_bf16_pairs(_dot(hid.astype(BF16), wdb_ref[...])))

    @pl.when(jnp.logical_not(live))
    def _():
        ys_ref[...] = jnp.zeros_like(ys_ref)


def _experts(block_e, n_used, xs, w_gate, w_up, w_down, layer):
    n_blocks = xs.shape[0] // (EXPERT_ROWS * ROW_TILES)
    rows = lambda i, be, nu: (i, 0)
    by_expert = lambda i, be, nu: (layer, be[i], 0, 0)
    grid_spec = pltpu.PrefetchScalarGridSpec(
        num_scalar_prefetch=2,
        grid=(n_blocks,),
        in_specs=[
            pl.BlockSpec((EXPERT_ROWS * ROW_TILES, LANES), rows),
            pl.BlockSpec((None, None, D_MODEL, D_FF_EXPERT), by_expert),
            pl.BlockSpec((None, None, D_MODEL, D_FF_EXPERT), by_expert),
            pl.BlockSpec((None, None, D_FF_EXPERT, D_MODEL), by_expert),
        ],
        out_specs=pl.BlockSpec((EXPERT_ROWS * ROW_TILES, LANES), rows),
        scratch_shapes=[
            pltpu.VMEM((D_MODEL, D_FF_EXPERT), BF16),
            pltpu.VMEM((D_MODEL, D_FF_EXPERT), BF16),
            pltpu.VMEM((D_FF_EXPERT, D_MODEL), BF16),
        ],
    )
    return pl.pallas_call(
        _expert_kernel,
        grid_spec=grid_spec,
        out_shape=jax.ShapeDtypeStruct(xs.shape, U32),
        compiler_params=_cparams(1, 40),
        name="moe_experts",
    )(block_e, n_used, xs, w_gate, w_up, w_down)


def _combine_kernel(slots_hbm, ys_hbm, x_ref, gate_ref, gfin_ref, *out_and_scratch, tiles_a):
    if tiles_a is None:
        x3_ref, rows_ref, slots_smem, idx_sem, row_sem = out_and_scratch
    else:
        ya_ref, yb_ref, rows_ref, slots_smem, idx_sem, row_sem = out_and_scratch
    step = pl.program_id(0)
    _load_slots(slots_hbm, step, slots_smem, idx_sem)

    def issue(group, c, carry):
        for k in range(2):
            slot = slots_smem[k * (SUBLANES // 2) + group, c]
            _row_copy(ys_hbm, slot, rows_ref.at[k], group * LANES + c, row_sem).start(priority=k)
        return carry

    for group in range(DISPATCH_TOKENS // LANES):
        lax.fori_loop(0, LANES, functools.partial(issue, group), 0, unroll=ISSUE_UNROLL)

    def drain(r, carry):
        _row_copy(ys_hbm, 0, rows_ref.at[0], 0, row_sem).wait()
        return carry

    lax.fori_loop(0, 2 * DISPATCH_TOKENS, drain, 0, unroll=64)

    y0 = _unpack_bf16_pairs(_load_token_rows(rows_ref.at[0]))
    y1 = _unpack_bf16_pairs(_load_token_rows(rows_ref.at[1]))
    x3 = x_ref[...] + gate_ref[:, 0:1] * y0 + gate_ref[:, 1:2] * y1
    if tiles_a is None:
        x3_ref[...] = x3
    else:
        y = _rms(x3, gfin_ref[...])

        @pl.when(step < tiles_a)
        def _():
            ya_ref[...] = y

        @pl.when(step >= tiles_a)
        def _():
            yb_ref[...] = y


def _combine(slots, ys, x2, gates, g_final, rows_a):
    n = x2.shape[0]
    row = lambda i: (i, 0)
    const = lambda i: (0, 0)
    tok = DISPATCH_TOKENS
    if rows_a is None:
        tiles_a = None
        out_specs = [pl.BlockSpec((tok, D_MODEL), row)]
        out_shape = [jax.ShapeDtypeStruct((n, D_MODEL), F32)]
    else:
        tiles_a = rows_a // tok
        out_specs = [pl.BlockSpec((tok, D_MODEL), lambda i: (jnp.minimum(i, tiles_a - 1), 0)),
                     pl.BlockSpec((tok, D_MODEL), lambda i: (jnp.maximum(i - tiles_a, 0), 0))]
        out_shape = [jax.ShapeDtypeStruct((rows_a, D_MODEL), F32), jax.ShapeDtypeStruct((n - rows_a, D_MODEL), F32)]
    out = pl.pallas_call(
        functools.partial(_combine_kernel, tiles_a=tiles_a),
        grid=(n // tok,),
        in_specs=[
            pl.BlockSpec(memory_space=pl.ANY),
            pl.BlockSpec(memory_space=pl.ANY),
            pl.BlockSpec((tok, D_MODEL), row),
            pl.BlockSpec((tok, LANES), row),
            pl.BlockSpec((1, D_MODEL), const),
        ],
        out_specs=out_specs,
        out_shape=out_shape,
        scratch_shapes=[
            pltpu.VMEM((2, tok * ROW_TILES, LANES), U32),
            pltpu.SMEM((SUBLANES, LANES), I32),
            pltpu.SemaphoreType.DMA,
            pltpu.SemaphoreType.DMA,
        ],
        compiler_params=_cparams(1, 32),
        name="moe_combine" if rows_a is None else "moe_combine_final",
    )(slots, ys, x2, gates, g_final)
    return out


def _rope_tables(seq):
    rows = seq // GRID_W
    row_idx = jnp.repeat(jnp.arange(rows, dtype=F32), GRID_W)
    col_idx = jnp.tile(jnp.arange(GRID_W, dtype=F32), rows)
    n_freq = HEAD_DIM // 4
    inv_freq = ROPE_THETA ** (-jnp.arange(n_freq, dtype=F32) / n_freq)
    ang_row = row_idx[:, None] * inv_freq
    ang_col = col_idx[:, None] * inv_freq
    cos = jnp.concatenate([jnp.cos(ang_row)] * 2 + [jnp.cos(ang_col)] * 2, axis=1)
    sin = jnp.concatenate([-jnp.sin(ang_row), jnp.sin(ang_row), -jnp.sin(ang_col), jnp.sin(ang_col)], axis=1)
    return jnp.tile(cos, (1, LANES // HEAD_DIM)), jnp.tile(sin, (1, LANES // HEAD_DIM))


def _segment_ones():
    head = np.arange(ATTN_WIDTH) // HEAD_DIM
    return jnp.asarray(head[:, None] == head[None, :], BF16)


def _router_weights(w_group, b_group, w_router, b_router):
    w = jnp.zeros((ROUTER_ROWS, D_MODEL), F32)
    w = w.at[0:N_GROUPS].set(w_group.T).at[SUBLANES:SUBLANES + N_EXPERTS].set(w_router.T)
    hi = w.astype(BF16)
    lo = (w - hi.astype(F32)).astype(BF16)
    b = jnp.full((ROUTER_ROWS,), NEG_BIG, F32).at[0:N_GROUPS].set(b_group)
    b = b.at[SUBLANES:SUBLANES + N_EXPERTS].set(b_router)
    return hi, lo, jnp.tile(b[:, None], (1, LANES))


def _slot_plan(idx, counts, n):
    cnt = counts[:, 0].astype(I32)
    padded = (cnt + EXPERT_ROWS - 1) // EXPERT_ROWS * EXPERT_ROWS
    pad_end = jnp.cumsum(padded)
    pad_start = pad_end - padded
    experts = jnp.arange(N_EXPERTS, dtype=I32)
    start_of = lambda e: jnp.sum(jnp.where(e[:, None] == experts[None, :], pad_start[None, :], 0), axis=1)
    slot0 = start_of(idx[0]) + idx[2]
    slot1 = start_of(idx[1]) + idx[3]
    n_tiles = n // DISPATCH_TOKENS
    per_k = SUBLANES // 2
    slots = jnp.stack([slot0, slot1]).reshape(2, n_tiles, per_k, LANES)
    slots = slots.transpose(1, 0, 2, 3).reshape(n_tiles, SUBLANES, LANES)
    n_blocks = (2 * n) // EXPERT_ROWS + N_EXPERTS
    block_first_row = jnp.arange(n_blocks, dtype=I32) * EXPERT_ROWS
    ends_passed = jnp.sum((pad_end[None, :] <= block_first_row[:, None]).astype(I32), axis=1)
    block_e = jnp.minimum(ends_passed, N_EXPERTS - 1)
    n_used = (pad_end[-1:] // EXPERT_ROWS).astype(I32)
    return slots, block_e, n_used, n_blocks


def kernel(x_prompt, x_sample, mem_prompt, mem_sample, g_mix, w_in, g_q, g_k, b_gate, w_attn_up, conv_w, conv_b, ln_conv_g, ln_conv_b, w_conv_out, w_out, g_cross, g_mem, w_xq, w_xkv, w_xo, g_ffn, w_group, b_group, w_router, b_router, w_e_gate, w_e_up, w_e_down, g_final):
    bp, seq, d = x_prompt.shape
    bs = x_sample.shape[0]
    assert x_sample.shape[1] == seq and d == D_MODEL and seq % GRID_W == 0
    batch = bp + bs
    mem_len = mem_prompt.shape[1]
    n = batch * seq
    depth = g_mix.shape[0]
    tm = min(512, seq)
    tq_attn = min(512, seq)
    assert seq % tm == 0 and n % DISPATCH_TOKENS == 0 and (2 * n) % EXPERT_ROWS == 0 and seq % CONV_CHUNK == 0
    assert (bp * seq) % DISPATCH_TOKENS == 0

    xa = x_prompt.reshape(bp * seq, d)
    xb = x_sample.reshape(bs * seq, d)
    mem = jnp.concatenate([mem_prompt.reshape(bp * mem_len, d), mem_sample.reshape(bs * mem_len, d)], axis=0)
    cos_t, sin_t = _rope_tables(seq)
    seg = _segment_ones()
    tri = jnp.asarray(np.triu(np.ones((tm, tm), np.float32), k=1), BF16)
    row2 = lambda a: a.reshape(1, -1).astype(F32)

    out = None
    for l in range(depth):
        q, k, v, u, gates = _inproj(
            xa, xb, n, row2(g_mix[l]), w_in[l].astype(BF16), row2(jnp.tile(g_q[l], N_HEADS)),
            row2(jnp.tile(g_k[l], N_KV_HEADS)), cos_t, sin_t, row2(b_gate[l]), seg, seq, tm)
        attn_o = _attention(q, k, v, batch, seq, tq_attn)
        conv_o = _conv_branch(u, conv_w[l], row2(conv_b[l]), row2(ln_conv_g[l]), row2(ln_conv_b[l]), batch, seq)
        x1, qx = _merge(xa, xb, n, attn_o, conv_o, gates, w_attn_up[l].astype(BF16), w_conv_out[l].astype(BF16),
                        w_out[l].astype(BF16), row2(g_cross[l]), w_xq[l].astype(BF16), tm)
        kx, vx = _mem_kv(mem, row2(g_mem[l]), w_xkv[l].astype(BF16), min(512, mem.shape[0]))
        wr_hi, wr_lo, r_bias = _router_weights(w_group[l], b_group[l], w_router[l], b_router[l])
        x2, hp, idx, gate_tm, counts = _cross_route(
            x1, qx, kx, vx, w_xo[l].astype(BF16), row2(g_ffn[l]), wr_hi, wr_lo, r_bias, tri,
            batch, seq, mem_len, tm)
        slots, block_e, n_used, n_blocks = _slot_plan(idx, counts, n)
        xs = _dispatch(slots, hp, jnp.zeros((n_blocks * EXPERT_ROWS * ROW_TILES, LANES), U32))
        ys = _experts(block_e, n_used, xs, w_e_gate, w_e_up, w_e_down, l)
        last = l == depth - 1
        out = _combine(slots, ys, x2, gate_tm, row2(g_final), bp * seq if last else None)
        xa = xb = out[0]
    return (out[0].reshape(bp, seq, d), out[1].reshape(bs, seq, d))
```

```python
import functools

import numpy as np
import jax
import jax.numpy as jnp
from jax import lax
from jax.experimental import pallas as pl
from jax.experimental.pallas import tpu as pltpu

F32 = jnp.float32
BF16 = jnp.bfloat16
U32 = jnp.uint32
I32 = jnp.int32

EPS = 1e-6
D_MODEL = 1024
GRID_W = 64
N_HEADS = 8
N_KV_HEADS = 2
HEAD_DIM = 64
ROPE_THETA = 10000.0
ATTN_WIDTH = N_HEADS * HEAD_DIM
KV_WIDTH = N_KV_HEADS * HEAD_DIM
CONV_WIDTH = 512
CONV_KERNEL = 31
N_XHEADS = 4
XHEAD_DIM = D_MODEL // N_XHEADS
N_GROUPS = 4
EXPERTS_PER_GROUP = 8
N_EXPERTS = N_GROUPS * EXPERTS_PER_GROUP
D_FF_EXPERT = 512

V7X_VMEM_BYTES = 64 * 1024 * 1024
LANES = 128
SUBLANES = 8
EXPERT_ROWS = 256
ROUTER_ROWS = 40
PACKED = D_MODEL // 2
NEG_BIG = float(np.finfo(np.float32).min)
LOG2E = 1.4426950408889634
ATTN_MIN_ROW_SUM = 2.0 ** -100
NT_DIMS = (((1,), (1,)), ((), ()))


def _cparams(n_axes, vmem_mib):
    return pltpu.CompilerParams(
        dimension_semantics=("arbitrary",) * n_axes,
        vmem_limit_bytes=min(vmem_mib * 1024 * 1024, V7X_VMEM_BYTES - 8 * 1024 * 1024))


def _dot(a, b):
    return jnp.dot(a, b, preferred_element_type=F32)


def _sigmoid(x):
    return 1.0 / (1.0 + jnp.exp(-x))


def _rms(x, g):
    return x * lax.rsqrt(jnp.mean(x * x, axis=-1, keepdims=True) + EPS) * g


def _pack_bf16_pairs(y):
    w = y.shape[1] // 2
    yb = y.astype(BF16).astype(F32)
    hi = lax.bitcast_convert_type(yb[:, :w], U32)
    lo = lax.bitcast_convert_type(yb[:, w:], U32)
    return (hi & jnp.uint32(0xFFFF0000)) | (lo >> 16)


def _unpack_bf16_pairs(p):
    hi = lax.bitcast_convert_type(p & jnp.uint32(0xFFFF0000), F32)
    lo = lax.bitcast_convert_type(p << 16, F32)
    return jnp.concatenate([hi, lo], axis=1)


ROW_TILES = PACKED // LANES


def _store_token_rows(ref, packed):
    t = packed.shape[0]
    for j in range(ROW_TILES):
        ref[pl.ds(j, t, stride=ROW_TILES), :] = packed[:, j * LANES:(j + 1) * LANES]


def _load_token_rows(ref):
    t = ref.shape[0] // ROW_TILES
    return jnp.concatenate([ref[pl.ds(j, t, stride=ROW_TILES), :] for j in range(ROW_TILES)], axis=1)


def _head_norm_rope(raw, gain, seg, cos, sin, scale):
    width = raw.shape[1]
    sq = raw * raw
    ss = _dot(sq.astype(BF16), seg)
    qn = raw * lax.rsqrt(ss * (1.0 / HEAD_DIM) + EPS) * gain
    reps = width // LANES
    cos_w = jnp.concatenate([cos] * reps, axis=1) if reps > 1 else cos
    sin_w = jnp.concatenate([sin] * reps, axis=1) if reps > 1 else sin
    lane = lax.broadcasted_iota(I32, qn.shape, 1)
    first_half = (lane & 16) == 0
    partner = jnp.where(first_half, pltpu.roll(qn, width - 16, 1), pltpu.roll(qn, 16, 1))
    return (qn * cos_w + partner * sin_w) * scale


def _select_rows(xa_ref, xb_ref, tiles_a):
    return jnp.where(pl.program_id(0) < tiles_a, xa_ref[...], xb_ref[...])


def _split_specs(tm, tiles_a):
    return [pl.BlockSpec((tm, D_MODEL), lambda i: (jnp.minimum(i, tiles_a - 1), 0)),
            pl.BlockSpec((tm, D_MODEL), lambda i: (jnp.maximum(i - tiles_a, 0), 0))]


def _inproj_kernel(xa_ref, xb_ref, gmix_ref, w_ref, gq_ref, gk_ref, cos_ref, sin_ref, bg_ref, seg_ref,
                   q_ref, k_ref, v_ref, u_ref, gate_ref, *, tiles_a):
    h = _rms(_select_rows(xa_ref, xb_ref, tiles_a), gmix_ref[...]).astype(BF16)
    cos = cos_ref[...]
    sin = sin_ref[...]
    c0 = 0
    q_raw = _dot(h, w_ref[:, c0:c0 + ATTN_WIDTH])
    q_ref[...] = _head_norm_rope(q_raw, gq_ref[...], seg_ref[...], cos, sin, HEAD_DIM ** -0.5 * LOG2E).astype(BF16)
    c0 += ATTN_WIDTH
    k_raw = _dot(h, w_ref[:, c0:c0 + KV_WIDTH])
    k_ref[...] = _head_norm_rope(k_raw, gk_ref[...], seg_ref[:KV_WIDTH, :KV_WIDTH], cos, sin, 1.0).astype(BF16)
    c0 += KV_WIDTH
    v_ref[...] = _dot(h, w_ref[:, c0:c0 + KV_WIDTH]).T.astype(BF16)
    c0 += KV_WIDTH
    c_a = _dot(h, w_ref[:, c0:c0 + CONV_WIDTH])
    c0 += CONV_WIDTH
    c_b = _dot(h, w_ref[:, c0:c0 + CONV_WIDTH])
    c0 += CONV_WIDTH
    u_ref[...] = (c_a * _sigmoid(c_b)).astype(BF16)
    chunk = 512
    for j in range(2 * D_MODEL // chunk):
        lg = _dot(h, w_ref[:, c0 + j * chunk:c0 + (j + 1) * chunk]) + bg_ref[:, j * chunk:(j + 1) * chunk]
        gate_ref[:, j * chunk:(j + 1) * chunk] = _sigmoid(lg).astype(BF16)


def _inproj(xa, xb, n, gmix, w_in, gq_t, gk_t, cos_t, sin_t, b_gate, seg, seq, tm):
    tiles_a = min(xa.shape[0], n) // tm
    in_cols = w_in.shape[1]
    n_seq_tiles = seq // tm
    row = lambda i: (i, 0)
    const = lambda i: (0, 0)
    pos = lambda i: (i % n_seq_tiles, 0)
    return pl.pallas_call(
        functools.partial(_inproj_kernel, tiles_a=tiles_a),
        grid=(n // tm,),
        in_specs=_split_specs(tm, tiles_a) + [
            pl.BlockSpec((1, D_MODEL), const),
            pl.BlockSpec((D_MODEL, in_cols), const),
            pl.BlockSpec((1, ATTN_WIDTH), const),
            pl.BlockSpec((1, KV_WIDTH), const),
            pl.BlockSpec((tm, LANES), pos),
            pl.BlockSpec((tm, LANES), pos),
            pl.BlockSpec((1, 2 * D_MODEL), const),
            pl.BlockSpec((ATTN_WIDTH, ATTN_WIDTH), const),
        ],
        out_specs=[
            pl.BlockSpec((tm, ATTN_WIDTH), row),
            pl.BlockSpec((tm, KV_WIDTH), row),
            pl.BlockSpec((KV_WIDTH, tm), lambda i: (i // n_seq_tiles, i % n_seq_tiles)),
            pl.BlockSpec((tm, CONV_WIDTH), row),
            pl.BlockSpec((tm, 2 * D_MODEL), row),
        ],
        out_shape=[
            jax.ShapeDtypeStruct((n, ATTN_WIDTH), BF16),
            jax.ShapeDtypeStruct((n, KV_WIDTH), BF16),
            jax.ShapeDtypeStruct((n // seq * KV_WIDTH, seq), BF16),
            jax.ShapeDtypeStruct((n, CONV_WIDTH), BF16),
            jax.ShapeDtypeStruct((n, 2 * D_MODEL), BF16),
        ],
        compiler_params=_cparams(1, 48),
        name="inproj",
    )(xa, xb, gmix, w_in, gq_t, gk_t, cos_t, sin_t, b_gate, seg)


def _attn_kernel(q_ref, k_ref, vt_ref, o_ref):
    tq = q_ref.shape[0]
    group = N_HEADS // N_KV_HEADS

    def run(exact_shift):
        k_norm = []
        if not exact_shift:
            for g in range(N_KV_HEADS):
                kf = k_ref[:, g * HEAD_DIM:(g + 1) * HEAD_DIM].astype(F32)
                k_sq = jnp.max(jnp.sum(kf * kf, axis=-1, keepdims=True), axis=0, keepdims=True)
                k_norm.append(jnp.sqrt(k_sq))
        outs, inv_sums, min_sum = [], [], None
        for hd in range(N_HEADS):
            g = hd // group
            qh = q_ref[:, hd * HEAD_DIM:(hd + 1) * HEAD_DIM]
            s = lax.dot_general(qh, k_ref[:, g * HEAD_DIM:(g + 1) * HEAD_DIM], NT_DIMS, preferred_element_type=F32)
            if exact_shift:
                row_shift = jnp.max(s, axis=-1, keepdims=True)
            else:
                qf = qh.astype(F32)
                row_shift = jnp.sqrt(jnp.sum(qf * qf, axis=-1, keepdims=True)) * k_norm[g]
            p = jnp.exp2(s - row_shift)
            l = jnp.sum(p, axis=-1, keepdims=True)
            outs.append(lax.dot_general(vt_ref[g * HEAD_DIM:(g + 1) * HEAD_DIM, :], p.astype(BF16), NT_DIMS,
                                        preferred_element_type=F32))
            inv_sums.append(jnp.broadcast_to(1.0 / l, (tq, HEAD_DIM)))
            min_sum = l if min_sum is None else jnp.minimum(min_sum, l)
        o = jnp.concatenate(outs, axis=0).T * jnp.concatenate(inv_sums, axis=1)
        o_ref[...] = o.astype(BF16)
        return min_sum

    min_sum = run(exact_shift=False)

    @pl.when(jnp.min(min_sum) < ATTN_MIN_ROW_SUM)
    def _():
        run(exact_shift=True)


def _attention(q, k, v, batch, seq, tq):
    n = q.shape[0]
    nq = seq // tq
    return pl.pallas_call(
        _attn_kernel,
        grid=(batch, nq),
        in_specs=[
            pl.BlockSpec((tq, ATTN_WIDTH), lambda b, i: (b * nq + i, 0)),
            pl.BlockSpec((seq, KV_WIDTH), lambda b, i: (b, 0)),
            pl.BlockSpec((KV_WIDTH, seq), lambda b, i: (b, 0)),
        ],
        out_specs=pl.BlockSpec((tq, ATTN_WIDTH), lambda b, i: (b * nq + i, 0)),
        out_shape=jax.ShapeDtypeStruct((n, ATTN_WIDTH), BF16),
        compiler_params=_cparams(2, 48),
        name="gqa_attention",
    )(q, k, v)


CONV_HALO = 16
CONV_CHUNK = 64
CONV_WINDOW = CONV_CHUNK + 2 * CONV_HALO
CONV_SPAN = CONV_CHUNK + (CONV_KERNEL // SUBLANES) * SUBLANES


def _conv_shift_matrix():
    m = np.zeros((SUBLANES * CONV_SPAN, CONV_WINDOW), np.float32)
    for res in range(SUBLANES):
        rows = np.arange(CONV_SPAN)
        m[res * CONV_SPAN + rows, rows + res] = 1.0
    return jnp.asarray(m, BF16)


def _conv_kernel(u_ref, shift_ref, w_ref, b_ref, g_ref, beta_ref, o_ref, pad_ref):
    seq = u_ref.shape[0]
    zeros = jnp.zeros((CONV_HALO, CONV_WIDTH), BF16)
    pad_ref[0:CONV_HALO, :] = zeros
    pad_ref[CONV_HALO + seq:CONV_HALO + seq + CONV_HALO, :] = zeros
    pad_ref[CONV_HALO:CONV_HALO + seq, :] = u_ref[...]
    shift = CONV_HALO - CONV_KERNEL // 2

    def chunk(c, carry):
        t0 = pl.multiple_of(c * CONV_CHUNK, CONV_CHUNK)
        win = pad_ref[pl.ds(t0, CONV_WINDOW), :]
        acc = jnp.zeros((CONV_CHUNK, CONV_WIDTH), F32)
        for res in range(SUBLANES):
            shifted = _dot(shift_ref[res * CONV_SPAN:(res + 1) * CONV_SPAN, :], win)
            for tap in range(CONV_KERNEL):
                if (tap + shift) % SUBLANES == res:
                    r0 = (tap + shift) - res
                    acc = acc + shifted[r0:r0 + CONV_CHUNK] * w_ref[tap:tap + 1, :]
        acc = acc + b_ref[...]
        mu = jnp.mean(acc, axis=-1, keepdims=True)
        xc = acc - mu
        y = xc * lax.rsqrt(jnp.mean(xc * xc, axis=-1, keepdims=True) + EPS) * g_ref[...] + beta_ref[...]
        o_ref[pl.ds(t0, CONV_CHUNK), :] = (y * _sigmoid(y)).astype(BF16)
        return carry

    lax.fori_loop(0, seq // CONV_CHUNK, chunk, 0, unroll=2)


def _conv_branch(u, conv_w, conv_b, ln_g, ln_b, batch, seq):
    n = u.shape[0]
    const = lambda b: (0, 0)
    return pl.pallas_call(
        _conv_kernel,
        grid=(batch,),
        in_specs=[
            pl.BlockSpec((seq, CONV_WIDTH), lambda b: (b, 0)),
            pl.BlockSpec((SUBLANES * CONV_SPAN, CONV_WINDOW), const),
            pl.BlockSpec((CONV_KERNEL, CONV_WIDTH), const),
            pl.BlockSpec((1, CONV_WIDTH), const),
            pl.BlockSpec((1, CONV_WIDTH), const),
            pl.BlockSpec((1, CONV_WIDTH), const),
        ],
        out_specs=pl.BlockSpec((seq, CONV_WIDTH), lambda b: (b, 0)),
        out_shape=jax.ShapeDtypeStruct((n, CONV_WIDTH), BF16),
        scratch_shapes=[pltpu.VMEM((seq + 2 * CONV_HALO, CONV_WIDTH), BF16)],
        compiler_params=_cparams(1, 32),
        name="conv_branch",
    )(u, _conv_shift_matrix(), conv_w, conv_b, ln_g, ln_b)


def _merge_kernel(xa_ref, xb_ref, a_ref, c_ref, gate_ref, wup_ref, wco_ref, wout_ref, gx_ref, wxq_ref,
                  x1_ref, qx_ref, *, tiles_a):
    attn = _dot(a_ref[...], wup_ref[...])
    conv = _dot(c_ref[...], wco_ref[...])
    merged = gate_ref[:, :D_MODEL].astype(F32) * attn + gate_ref[:, D_MODEL:].astype(F32) * conv
    x1 = _select_rows(xa_ref, xb_ref, tiles_a) + _dot(merged.astype(BF16), wout_ref[...])
    x1_ref[...] = x1
    h = _rms(x1, gx_ref[...]).astype(BF16)
    qx_ref[...] = (_dot(h, wxq_ref[...]) * (XHEAD_DIM ** -0.5)).astype(BF16)


def _merge(xa, xb, n, attn_o, conv_o, gates, w_up, w_co, w_out, g_cross, w_xq, tm):
    tiles_a = min(xa.shape[0], n) // tm
    row = lambda i: (i, 0)
    const = lambda i: (0, 0)
    return pl.pallas_call(
        functools.partial(_merge_kernel, tiles_a=tiles_a),
        grid=(n // tm,),
        in_specs=_split_specs(tm, tiles_a) + [
            pl.BlockSpec((tm, ATTN_WIDTH), row),
            pl.BlockSpec((tm, CONV_WIDTH), row),
            pl.BlockSpec((tm, 2 * D_MODEL), row),
            pl.BlockSpec((ATTN_WIDTH, D_MODEL), const),
            pl.BlockSpec((CONV_WIDTH, D_MODEL), const),
            pl.BlockSpec((D_MODEL, D_MODEL), const),
            pl.BlockSpec((1, D_MODEL), const),
            pl.BlockSpec((D_MODEL, D_MODEL), const),
        ],
        out_specs=[pl.BlockSpec((tm, D_MODEL), row), pl.BlockSpec((tm, D_MODEL), row)],
        out_shape=[jax.ShapeDtypeStruct((n, D_MODEL), F32), jax.ShapeDtypeStruct((n, D_MODEL), BF16)],
        compiler_params=_cparams(1, 48),
        name="merge_out_xq",
    )(xa, xb, attn_o, conv_o, gates, w_up, w_co, w_out, g_cross, w_xq)


def _memkv_kernel(m_ref, g_ref, w_ref, k_ref, v_ref):
    h = _rms(m_ref[...], g_ref[...]).astype(BF16)
    k_ref[...] = _dot(h, w_ref[:, :D_MODEL]).astype(BF16)
    v_ref[...] = _dot(h, w_ref[:, D_MODEL:]).astype(BF16)


def _mem_kv(mem, g_mem, w_xkv, tm):
    n = mem.shape[0]
    row = lambda i: (i, 0)
    const = lambda i: (0, 0)
    return pl.pallas_call(
        _memkv_kernel,
        grid=(n // tm,),
        in_specs=[
            pl.BlockSpec((tm, D_MODEL), row),
            pl.BlockSpec((1, D_MODEL), const),
            pl.BlockSpec((D_MODEL, 2 * D_MODEL), const),
        ],
        out_specs=[pl.BlockSpec((tm, D_MODEL), row), pl.BlockSpec((tm, D_MODEL), row)],
        out_shape=[jax.ShapeDtypeStruct((n, D_MODEL), BF16), jax.ShapeDtypeStruct((n, D_MODEL), BF16)],
        compiler_params=_cparams(1, 32),
        name="mem_kv",
    )(mem, g_mem, w_xkv)


def _route(lg, tri, base_ref):
    t = lg.shape[1]
    gl = lg[0:SUBLANES]
    gmax = jnp.max(gl, axis=0, keepdims=True)
    p_top = 1.0 / jnp.sum(jnp.exp(gl - gmax), axis=0, keepdims=True)
    iota_e = lax.broadcasted_iota(I32, (SUBLANES, t), 0).astype(F32)
    none = float(SUBLANES)
    grp = jnp.min(jnp.where(gl == gmax, iota_e, none), axis=0, keepdims=True)
    el = lg[SUBLANES:SUBLANES + N_EXPERTS]
    sel = el[0:EXPERTS_PER_GROUP]
    for g in range(1, N_GROUPS):
        sel = jnp.where(grp == float(g), el[g * EXPERTS_PER_GROUP:(g + 1) * EXPERTS_PER_GROUP], sel)
    v0 = jnp.max(sel, axis=0, keepdims=True)
    i0 = jnp.min(jnp.where(sel == v0, iota_e, none), axis=0, keepdims=True)
    rest = jnp.where(iota_e == i0, NEG_BIG, sel)
    v1 = jnp.max(rest, axis=0, keepdims=True)
    i1 = jnp.min(jnp.where(rest == v1, iota_e, none), axis=0, keepdims=True)
    ratio = jnp.exp(v1 - v0)
    g0 = p_top / (1.0 + ratio)
    g1 = p_top * ratio / (1.0 + ratio)
    e0 = grp * float(EXPERTS_PER_GROUP) + i0
    e1 = grp * float(EXPERTS_PER_GROUP) + i1
    iota_all = lax.broadcasted_iota(I32, (N_EXPERTS, t), 0).astype(F32)
    hit0 = iota_all == e0
    hit1 = iota_all == e1
    onehot = jnp.where(hit0 | hit1, 1.0, 0.0)
    before = _dot(onehot.astype(BF16), tri) + base_ref[:, 0:1]
    r0 = jnp.sum(jnp.where(hit0, before, 0.0), axis=0, keepdims=True)
    r1 = jnp.sum(jnp.where(hit1, before, 0.0), axis=0, keepdims=True)
    base_ref[...] = base_ref[...] + jnp.sum(onehot, axis=1, keepdims=True)
    return e0.astype(I32), e1.astype(I32), r0.astype(I32), r1.astype(I32), g0, g1


def _cross_kernel(x_ref, q_ref, k_ref, v_ref, wo_ref, gffn_ref, wrh_ref, wrl_ref, rb_ref, tri_ref,
                  x2_ref, hp_ref, idx_ref, gate_ref, cnt_ref, base_ref):
    @pl.when((pl.program_id(0) == 0) & (pl.program_id(1) == 0))
    def _():
        base_ref[...] = jnp.zeros_like(base_ref)

    heads = []
    for hd in range(N_XHEADS):
        sl = slice(hd * XHEAD_DIM, (hd + 1) * XHEAD_DIM)
        s = lax.dot_general(q_ref[:, sl], k_ref[:, sl], NT_DIMS, preferred_element_type=F32)
        m = jnp.max(s, axis=-1, keepdims=True)
        p = jnp.exp(s - m)
        l = jnp.sum(p, axis=-1, keepdims=True)
        heads.append(_dot(p.astype(BF16), v_ref[:, sl]) / l)
    o = jnp.concatenate(heads, axis=1).astype(BF16)
    x2 = x_ref[...] + _dot(o, wo_ref[...])
    x2_ref[...] = x2
    h = _rms(x2, gffn_ref[...])
    h_hi = h.astype(BF16)
    _store_token_rows(hp_ref, _pack_bf16_pairs(h))
    h_lo = (h - h_hi.astype(F32)).astype(BF16)
    lg = (lax.dot_general(wrh_ref[...], h_hi, NT_DIMS, preferred_element_type=F32)
          + lax.dot_general(wrh_ref[...], h_lo, NT_DIMS, preferred_element_type=F32)
          + lax.dot_general(wrl_ref[...], h_hi, NT_DIMS, preferred_element_type=F32)) + rb_ref[:, 0:1]
    e0, e1, r0, r1, g0, g1 = _route(lg, tri_ref[...], base_ref)
    t = lg.shape[1]
    idx_ref[...] = jnp.concatenate([e0, e1, r0, r1, jnp.zeros((SUBLANES - 4, t), I32)], axis=0)
    gl = jnp.concatenate([g0, g1, jnp.zeros((LANES - 2, t), F32)], axis=0)
    gate_ref[...] = gl.T
    cnt_ref[...] = base_ref[...]


def _cross_route(x1, qx, kx, vx, w_xo, g_ffn, wr_hi, wr_lo, r_bias, tri, batch, seq, mem_len, tq):
    n = x1.shape[0]
    nq = seq // tq
    row = lambda b, i: (b * nq + i, 0)
    per_b = lambda b, i: (b, 0)
    const = lambda b, i: (0, 0)
    return pl.pallas_call(
        _cross_kernel,
        grid=(batch, nq),
        in_specs=[
            pl.BlockSpec((tq, D_MODEL), row),
            pl.BlockSpec((tq, D_MODEL), row),
            pl.BlockSpec((mem_len, D_MODEL), per_b),
            pl.BlockSpec((mem_len, D_MODEL), per_b),
            pl.BlockSpec((D_MODEL, D_MODEL), const),
            pl.BlockSpec((1, D_MODEL), const),
            pl.BlockSpec((ROUTER_ROWS, D_MODEL), const),
            pl.BlockSpec((ROUTER_ROWS, D_MODEL), const),
            pl.BlockSpec((ROUTER_ROWS, LANES), const),
            pl.BlockSpec((tq, tq), const),
        ],
        out_specs=[
            pl.BlockSpec((tq, D_MODEL), row),
            pl.BlockSpec((tq * ROW_TILES, LANES), row),
            pl.BlockSpec((SUBLANES, tq), lambda b, i: (0, b * nq + i)),
            pl.BlockSpec((tq, LANES), row),
            pl.BlockSpec((N_EXPERTS, LANES), const),
        ],
        out_shape=[
            jax.ShapeDtypeStruct((n, D_MODEL), F32),
            jax.ShapeDtypeStruct((n * ROW_TILES, LANES), U32),
            jax.ShapeDtypeStruct((SUBLANES, n), I32),
            jax.ShapeDtypeStruct((n, LANES), F32),
            jax.ShapeDtypeStruct((N_EXPERTS, LANES), F32),
        ],
        scratch_shapes=[pltpu.VMEM((N_EXPERTS, LANES), F32)],
        compiler_params=_cparams(2, 48),
        name="cross_attn_router",
    )(x1, qx, kx, vx, w_xo, g_ffn, wr_hi, wr_lo, r_bias, tri)


DISPATCH_TOKENS = 512
ISSUE_UNROLL = 8


def _row_copy(src_ref, src_tok, dst_ref, dst_tok, sem):
    return pltpu.make_async_copy(src_ref.at[pl.ds(src_tok * ROW_TILES, ROW_TILES)],
                                 dst_ref.at[pl.ds(dst_tok * ROW_TILES, ROW_TILES)], sem)


def _load_slots(slots_hbm, step, slots_smem, sem):
    cp = pltpu.make_async_copy(slots_hbm.at[step], slots_smem, sem)
    cp.start()
    cp.wait()


def _dispatch_kernel(zero_blocks, slots_hbm, hp_ref, xs_out, slots_smem, zero_ref, idx_sem, row_sem, zero_sem):
    block_rows = EXPERT_ROWS * ROW_TILES

    def zero_copy(j):
        return pltpu.make_async_copy(zero_ref, xs_out.at[pl.ds(zero_blocks[j] * block_rows, block_rows)], zero_sem)

    @pl.when(pl.program_id(0) == 0)
    def _():
        zero_ref[...] = jnp.zeros_like(zero_ref)
        for j in range(zero_blocks.shape[0]):
            zero_copy(j).start()
        for j in range(zero_blocks.shape[0]):
            zero_copy(j).wait()

    _load_slots(slots_hbm, pl.program_id(0), slots_smem, idx_sem)

    def issue(group, c, carry):
        for k in range(2):
            slot = slots_smem[k * (SUBLANES // 2) + group, c]
            _row_copy(hp_ref, group * LANES + c, xs_out, slot, row_sem).start(priority=k)
        return carry

    for group in range(DISPATCH_TOKENS // LANES):
        lax.fori_loop(0, LANES, functools.partial(issue, group), 0, unroll=ISSUE_UNROLL)

    def drain(r, carry):
        _row_copy(hp_ref, 0, xs_out, 0, row_sem).wait()
        return carry

    lax.fori_loop(0, 2 * DISPATCH_TOKENS, drain, 0, unroll=64)


def _dispatch(zero_blocks, slots, hp, n_blocks):
    n = hp.shape[0] // ROW_TILES
    block_rows = EXPERT_ROWS * ROW_TILES
    grid_spec = pltpu.PrefetchScalarGridSpec(
        num_scalar_prefetch=1,
        grid=(n // DISPATCH_TOKENS,),
        in_specs=[
            pl.BlockSpec(memory_space=pl.ANY),
            pl.BlockSpec((DISPATCH_TOKENS * ROW_TILES, LANES), lambda i, zb: (i, 0)),
        ],
        out_specs=pl.BlockSpec(memory_space=pl.ANY),
        scratch_shapes=[
            pltpu.SMEM((SUBLANES, LANES), I32),
            pltpu.VMEM((block_rows, LANES), U32),
            pltpu.SemaphoreType.DMA,
            pltpu.SemaphoreType.DMA,
            pltpu.SemaphoreType.DMA,
        ],
    )
    return pl.pallas_call(
        _dispatch_kernel,
        grid_spec=grid_spec,
        out_shape=jax.ShapeDtypeStruct((n_blocks * block_rows, LANES), U32),
        compiler_params=_cparams(1, 16),
        name="moe_dispatch",
    )(zero_blocks, slots, hp)


def _expert_kernel(be_ref, nused_ref, xs_ref, wg_ref, wu_ref, wd_ref, ys_ref, wgb_ref, wub_ref, wdb_ref):
    step = pl.program_id(0)
    live = step < nused_ref[0]

    @pl.when((step == 0) | (be_ref[step] != be_ref[jnp.maximum(step - 1, 0)]))
    def _():
        wgb_ref[...] = wg_ref[...].astype(BF16)
        wub_ref[...] = wu_ref[...].astype(BF16)
        wdb_ref[...] = wd_ref[...].astype(BF16)

    @pl.when(live)
    def _():
        x = _unpack_bf16_pairs(_load_token_rows(xs_ref)).astype(BF16)
        gate = _dot(x, wgb_ref[...])
        hid = (gate * _sigmoid(gate)) * _dot(x, wub_ref[...])
        _store_token_rows(ys_ref, _pack_bf16_pairs(_dot(hid.astype(BF16), wdb_ref[...])))

    @pl.when(jnp.logical_not(live))
    def _():
        ys_ref[...] = jnp.zeros_like(ys_ref)


def _experts(block_e, n_used, xs, w_gate, w_up, w_down, layer):
    n_blocks = xs.shape[0] // (EXPERT_ROWS * ROW_TILES)
    rows = lambda i, be, nu: (i, 0)
    by_expert = lambda i, be, nu: (layer, be[i], 0, 0)
    grid_spec = pltpu.PrefetchScalarGridSpec(
        num_scalar_prefetch=2,
        grid=(n_blocks,),
        in_specs=[
            pl.BlockSpec((EXPERT_ROWS * ROW_TILES, LANES), rows),
            pl.BlockSpec((None, None, D_MODEL, D_FF_EXPERT), by_expert),
            pl.BlockSpec((None, None, D_MODEL, D_FF_EXPERT), by_expert),
            pl.BlockSpec((None, None, D_FF_EXPERT, D_MODEL), by_expert),
        ],
        out_specs=pl.BlockSpec((EXPERT_ROWS * ROW_TILES, LANES), rows),
        scratch_shapes=[
            pltpu.VMEM((D_MODEL, D_FF_EXPERT), BF16),
            pltpu.VMEM((D_MODEL, D_FF_EXPERT), BF16),
            pltpu.VMEM((D_FF_EXPERT, D_MODEL), BF16),
        ],
    )
    return pl.pallas_call(
        _expert_kernel,
        grid_spec=grid_spec,
        out_shape=jax.ShapeDtypeStruct(xs.shape, U32),
        compiler_params=_cparams(1, 40),
        name="moe_experts",
    )(block_e, n_used, xs, w_gate, w_up, w_down)


def _combine_kernel(slots_hbm, ys_hbm, x_ref, gate_ref, gfin_ref, *out_and_scratch, tiles_a):
    if tiles_a is None:
        x3_ref, rows_ref, slots_smem, idx_sem, row_sem = out_and_scratch
    else:
        ya_ref, yb_ref, rows_ref, slots_smem, idx_sem, row_sem = out_and_scratch
    step = pl.program_id(0)
    _load_slots(slots_hbm, step, slots_smem, idx_sem)

    def issue(group, c, carry):
        for k in range(2):
            slot = slots_smem[k * (SUBLANES // 2) + group, c]
            _row_copy(ys_hbm, slot, rows_ref.at[k], group * LANES + c, row_sem).start(priority=k)
        return carry

    for group in range(DISPATCH_TOKENS // LANES):
        lax.fori_loop(0, LANES, functools.partial(issue, group), 0, unroll=ISSUE_UNROLL)

    def drain(r, carry):
        _row_copy(ys_hbm, 0, rows_ref.at[0], 0, row_sem).wait()
        return carry

    lax.fori_loop(0, 2 * DISPATCH_TOKENS, drain, 0, unroll=64)

    y0 = _unpack_bf16_pairs(_load_token_rows(rows_ref.at[0]))
    y1 = _unpack_bf16_pairs(_load_token_rows(rows_ref.at[1]))
    x3 = x_ref[...] + gate_ref[:, 0:1] * y0 + gate_ref[:, 1:2] * y1
    if tiles_a is None:
        x3_ref[...] = x3
    else:
        y = _rms(x3, gfin_ref[...])

        @pl.when(step < tiles_a)
        def _():
            ya_ref[...] = y

        @pl.when(step >= tiles_a)
        def _():
            yb_ref[...] = y


def _combine(slots, ys, x2, gates, g_final, rows_a):
    n = x2.shape[0]
    row = lambda i: (i, 0)
    const = lambda i: (0, 0)
    tok = DISPATCH_TOKENS
    if rows_a is None:
        tiles_a = None
        out_specs = [pl.BlockSpec((tok, D_MODEL), row)]
        out_shape = [jax.ShapeDtypeStruct((n, D_MODEL), F32)]
    else:
        tiles_a = rows_a // tok
        out_specs = [pl.BlockSpec((tok, D_MODEL), lambda i: (jnp.minimum(i, tiles_a - 1), 0)),
                     pl.BlockSpec((tok, D_MODEL), lambda i: (jnp.maximum(i - tiles_a, 0), 0))]
        out_shape = [jax.ShapeDtypeStruct((rows_a, D_MODEL), F32), jax.ShapeDtypeStruct((n - rows_a, D_MODEL), F32)]
    out = pl.pallas_call(
        functools.partial(_combine_kernel, tiles_a=tiles_a),
        grid=(n // tok,),
        in_specs=[
            pl.BlockSpec(memory_space=pl.ANY),
            pl.BlockSpec(memory_space=pl.ANY),
            pl.BlockSpec((tok, D_MODEL), row),
            pl.BlockSpec((tok, LANES), row),
            pl.BlockSpec((1, D_MODEL), const),
        ],
        out_specs=out_specs,
        out_shape=out_shape,
        scratch_shapes=[
            pltpu.VMEM((2, tok * ROW_TILES, LANES), U32),
            pltpu.SMEM((SUBLANES, LANES), I32),
            pltpu.SemaphoreType.DMA,
            pltpu.SemaphoreType.DMA,
        ],
        compiler_params=_cparams(1, 32),
        name="moe_combine" if rows_a is None else "moe_combine_final",
    )(slots, ys, x2, gates, g_final)
    return out


def _rope_tables(seq):
    rows = seq // GRID_W
    row_idx = jnp.repeat(jnp.arange(rows, dtype=F32), GRID_W)
    col_idx = jnp.tile(jnp.arange(GRID_W, dtype=F32), rows)
    n_freq = HEAD_DIM // 4
    inv_freq = ROPE_THETA ** (-jnp.arange(n_freq, dtype=F32) / n_freq)
    ang_row = row_idx[:, None] * inv_freq
    ang_col = col_idx[:, None] * inv_freq
    cos = jnp.concatenate([jnp.cos(ang_row)] * 2 + [jnp.cos(ang_col)] * 2, axis=1)
    sin = jnp.concatenate([-jnp.sin(ang_row), jnp.sin(ang_row), -jnp.sin(ang_col), jnp.sin(ang_col)], axis=1)
    return jnp.tile(cos, (1, LANES // HEAD_DIM)), jnp.tile(sin, (1, LANES // HEAD_DIM))


def _segment_ones():
    head = np.arange(ATTN_WIDTH) // HEAD_DIM
    return jnp.asarray(head[:, None] == head[None, :], BF16)


def _router_weights(w_group, b_group, w_router, b_router):
    w = jnp.zeros((ROUTER_ROWS, D_MODEL), F32)
    w = w.at[0:N_GROUPS].set(w_group.T).at[SUBLANES:SUBLANES + N_EXPERTS].set(w_router.T)
    hi = w.astype(BF16)
    lo = (w - hi.astype(F32)).astype(BF16)
    b = jnp.full((ROUTER_ROWS,), NEG_BIG, F32).at[0:N_GROUPS].set(b_group)
    b = b.at[SUBLANES:SUBLANES + N_EXPERTS].set(b_router)
    return hi, lo, jnp.tile(b[:, None], (1, LANES))


def _slot_plan(idx, counts, n):
    cnt = counts[:, 0].astype(I32)
    padded = (cnt + EXPERT_ROWS - 1) // EXPERT_ROWS * EXPERT_ROWS
    pad_end = jnp.cumsum(padded)
    pad_start = pad_end - padded
    experts = jnp.arange(N_EXPERTS, dtype=I32)
    start_of = lambda e: jnp.sum(jnp.where(e[:, None] == experts[None, :], pad_start[None, :], 0), axis=1)
    slot0 = start_of(idx[0]) + idx[2]
    slot1 = start_of(idx[1]) + idx[3]
    n_tiles = n // DISPATCH_TOKENS
    per_k = SUBLANES // 2
    slots = jnp.stack([slot0, slot1]).reshape(2, n_tiles, per_k, LANES)
    slots = slots.transpose(1, 0, 2, 3).reshape(n_tiles, SUBLANES, LANES)
    n_blocks = (2 * n) // EXPERT_ROWS + N_EXPERTS
    block_first_row = jnp.arange(n_blocks, dtype=I32) * EXPERT_ROWS
    ends_passed = jnp.sum((pad_end[None, :] <= block_first_row[:, None]).astype(I32), axis=1)
    block_e = jnp.minimum(ends_passed, N_EXPERTS - 1)
    n_used = (pad_end[-1:] // EXPERT_ROWS).astype(I32)
    last_block = jnp.where(padded > 0, pad_end // EXPERT_ROWS - 1, n_blocks - 1)
    unused = jnp.minimum(n_used[0] + jnp.arange(N_EXPERTS, dtype=I32), n_blocks - 1)
    need = jnp.zeros((n_blocks,), I32).at[jnp.concatenate([last_block, unused])].set(1)
    zero_blocks = jnp.argsort(1 - need, stable=True)[:min(2 * N_EXPERTS, n_blocks)].astype(I32)
    return slots, block_e, n_used, n_blocks, zero_blocks


def kernel(x_prompt, x_sample, mem_prompt, mem_sample, g_mix, w_in, g_q, g_k, b_gate, w_attn_up, conv_w, conv_b, ln_conv_g, ln_conv_b, w_conv_out, w_out, g_cross, g_mem, w_xq, w_xkv, w_xo, g_ffn, w_group, b_group, w_router, b_router, w_e_gate, w_e_up, w_e_down, g_final):
    bp, seq, d = x_prompt.shape
    bs = x_sample.shape[0]
    assert x_sample.shape[1] == seq and d == D_MODEL and seq % GRID_W == 0
    batch = bp + bs
    mem_len = mem_prompt.shape[1]
    n = batch * seq
    depth = g_mix.shape[0]
    tm = min(512, seq)
    tq_attn = min(512, seq)
    assert seq % tm == 0 and n % DISPATCH_TOKENS == 0 and (2 * n) % EXPERT_ROWS == 0 and seq % CONV_CHUNK == 0
    assert (bp * seq) % DISPATCH_TOKENS == 0

    xa = x_prompt.reshape(bp * seq, d)
    xb = x_sample.reshape(bs * seq, d)
    mem = jnp.concatenate([mem_prompt.reshape(bp * mem_len, d), mem_sample.reshape(bs * mem_len, d)], axis=0)
    cos_t, sin_t = _rope_tables(seq)
    seg = _segment_ones()
    tri = jnp.asarray(np.triu(np.ones((tm, tm), np.float32), k=1), BF16)
    row2 = lambda a: a.reshape(1, -1).astype(F32)

    out = None
    for l in range(depth):
        q, k, v, u, gates = _inproj(
            xa, xb, n, row2(g_mix[l]), w_in[l].astype(BF16), row2(jnp.tile(g_q[l], N_HEADS)),
            row2(jnp.tile(g_k[l], N_KV_HEADS)), cos_t, sin_t, row2(b_gate[l]), seg, seq, tm)
        attn_o = _attention(q, k, v, batch, seq, tq_attn)
        conv_o = _conv_branch(u, conv_w[l], row2(conv_b[l]), row2(ln_conv_g[l]), row2(ln_conv_b[l]), batch, seq)
        x1, qx = _merge(xa, xb, n, attn_o, conv_o, gates, w_attn_up[l].astype(BF16), w_conv_out[l].astype(BF16),
                        w_out[l].astype(BF16), row2(g_cross[l]), w_xq[l].astype(BF16), tm)
        kx, vx = _mem_kv(mem, row2(g_mem[l]), w_xkv[l].astype(BF16), min(512, mem.shape[0]))
        wr_hi, wr_lo, r_bias = _router_weights(w_group[l], b_group[l], w_router[l], b_router[l])
        x2, hp, idx, gate_tm, counts = _cross_route(
            x1, qx, kx, vx, w_xo[l].astype(BF16), row2(g_ffn[l]), wr_hi, wr_lo, r_bias, tri,
            batch, seq, mem_len, tm)
        slots, block_e, n_used, n_blocks, zero_blocks = _slot_plan(idx, counts, n)
        xs = _dispatch(zero_blocks, slots, hp, n_blocks)
        ys = _experts(block_e, n_used, xs, w_e_gate, w_e_up, w_e_down, l)
        last = l == depth - 1
        out = _combine(slots, ys, x2, gate_tm, row2(g_final), bp * seq if last else None)
        xa = xb = out[0]
    return (out[0].reshape(bp, seq, d), out[1].reshape(bs, seq, d))
```

```python
import functools

import numpy as np
import jax
import jax.numpy as jnp
from jax import lax
from jax.experimental import pallas as pl
from jax.experimental.pallas import tpu as pltpu

F32 = jnp.float32
BF16 = jnp.bfloat16
U32 = jnp.uint32
I32 = jnp.int32

EPS = 1e-6
D_MODEL = 1024
GRID_W = 64
N_HEADS = 8
N_KV_HEADS = 2
HEAD_DIM = 64
ROPE_THETA = 10000.0
ATTN_WIDTH = N_HEADS * HEAD_DIM
KV_WIDTH = N_KV_HEADS * HEAD_DIM
CONV_WIDTH = 512
CONV_KERNEL = 31
N_XHEADS = 4
XHEAD_DIM = D_MODEL // N_XHEADS
N_GROUPS = 4
EXPERTS_PER_GROUP = 8
N_EXPERTS = N_GROUPS * EXPERTS_PER_GROUP
D_FF_EXPERT = 512

V7X_VMEM_BYTES = 64 * 1024 * 1024
LANES = 128
SUBLANES = 8
EXPERT_ROWS = 512
ROUTER_ROWS = 40
PACKED = D_MODEL // 2
NEG_BIG = float(np.finfo(np.float32).min)
LOG2E = 1.4426950408889634
ATTN_MIN_ROW_SUM = 2.0 ** -100
NT_DIMS = (((1,), (1,)), ((), ()))


def _cparams(n_axes, vmem_mib, flags=None):
    return pltpu.CompilerParams(
        dimension_semantics=("arbitrary",) * n_axes,
        vmem_limit_bytes=min(vmem_mib * 1024 * 1024, V7X_VMEM_BYTES - 8 * 1024 * 1024),
        flags=flags)


def _dot(a, b):
    return jnp.dot(a, b, preferred_element_type=F32)


def _sigmoid(x):
    return 1.0 / (1.0 + jnp.exp(-x))


def _rms(x, g):
    return x * lax.rsqrt(jnp.mean(x * x, axis=-1, keepdims=True) + EPS) * g


def _pack_bf16_pairs(y):
    w = y.shape[1] // 2
    yb = y.astype(BF16).astype(F32)
    hi = lax.bitcast_convert_type(yb[:, :w], U32)
    lo = lax.bitcast_convert_type(yb[:, w:], U32)
    return (hi & jnp.uint32(0xFFFF0000)) | (lo >> 16)


def _unpack_bf16_pairs(p):
    hi = lax.bitcast_convert_type(p & jnp.uint32(0xFFFF0000), F32)
    lo = lax.bitcast_convert_type(p << 16, F32)
    return jnp.concatenate([hi, lo], axis=1)


ROW_TILES = PACKED // LANES


def _store_token_rows(ref, packed):
    t = packed.shape[0]
    for j in range(ROW_TILES):
        ref[pl.ds(j, t, stride=ROW_TILES), :] = packed[:, j * LANES:(j + 1) * LANES]


def _load_token_rows(ref):
    t = ref.shape[0] // ROW_TILES
    return jnp.concatenate([ref[pl.ds(j, t, stride=ROW_TILES), :] for j in range(ROW_TILES)], axis=1)


def _head_norm_rope(raw, gain, seg, cos, sin, scale):
    width = raw.shape[1]
    sq = raw * raw
    ss = _dot(sq.astype(BF16), seg)
    qn = raw * lax.rsqrt(ss * (1.0 / HEAD_DIM) + EPS) * gain
    reps = width // LANES
    cos_w = jnp.concatenate([cos] * reps, axis=1) if reps > 1 else cos
    sin_w = jnp.concatenate([sin] * reps, axis=1) if reps > 1 else sin
    lane = lax.broadcasted_iota(I32, qn.shape, 1)
    first_half = (lane & 16) == 0
    partner = jnp.where(first_half, pltpu.roll(qn, width - 16, 1), pltpu.roll(qn, 16, 1))
    return (qn * cos_w + partner * sin_w) * scale


def _select_rows(xa_ref, xb_ref, tiles_a):
    return jnp.where(pl.program_id(0) < tiles_a, xa_ref[...], xb_ref[...])


def _split_specs(tm, tiles_a):
    return [pl.BlockSpec((tm, D_MODEL), lambda i: (jnp.minimum(i, tiles_a - 1), 0)),
            pl.BlockSpec((tm, D_MODEL), lambda i: (jnp.maximum(i - tiles_a, 0), 0))]


def _inproj_kernel(xa_ref, xb_ref, gmix_ref, w_ref, gq_ref, gk_ref, cos_ref, sin_ref, bg_ref, seg_ref,
                   q_ref, k_ref, v_ref, u_ref, gate_ref, *, tiles_a):
    x = _select_rows(xa_ref, xb_ref, tiles_a)
    h = (x * gmix_ref[...]).astype(BF16)
    r = lax.rsqrt(jnp.mean(x * x, axis=-1, keepdims=True) + EPS)
    cos = cos_ref[...]
    sin = sin_ref[...]
    c0 = 0
    q_raw = _dot(h, w_ref[:, c0:c0 + ATTN_WIDTH]) * r
    q_ref[...] = _head_norm_rope(q_raw, gq_ref[...], seg_ref[...], cos, sin, HEAD_DIM ** -0.5 * LOG2E).astype(BF16)
    c0 += ATTN_WIDTH
    k_raw = _dot(h, w_ref[:, c0:c0 + KV_WIDTH]) * r
    k_ref[...] = _head_norm_rope(k_raw, gk_ref[...], seg_ref[:KV_WIDTH, :KV_WIDTH], cos, sin, 1.0).astype(BF16)
    c0 += KV_WIDTH
    v_ref[...] = (_dot(h, w_ref[:, c0:c0 + KV_WIDTH]) * r).T.astype(BF16)
    c0 += KV_WIDTH
    c_a = _dot(h, w_ref[:, c0:c0 + CONV_WIDTH]) * r
    c0 += CONV_WIDTH
    c_b = _dot(h, w_ref[:, c0:c0 + CONV_WIDTH]) * r
    c0 += CONV_WIDTH
    u_ref[...] = (c_a * _sigmoid(c_b)).astype(BF16)
    chunk = 512
    for j in range(2 * D_MODEL // chunk):
        lg = _dot(h, w_ref[:, c0 + j * chunk:c0 + (j + 1) * chunk]) * r + bg_ref[:, j * chunk:(j + 1) * chunk]
        gate_ref[:, j * chunk:(j + 1) * chunk] = _sigmoid(lg).astype(BF16)


def _inproj(xa, xb, n, gmix, w_in, gq_t, gk_t, cos_t, sin_t, b_gate, seg, seq, tm):
    tiles_a = min(xa.shape[0], n) // tm
    in_cols = w_in.shape[1]
    n_seq_tiles = seq // tm
    row = lambda i: (i, 0)
    const = lambda i: (0, 0)
    pos = lambda i: (i % n_seq_tiles, 0)
    return pl.pallas_call(
        functools.partial(_inproj_kernel, tiles_a=tiles_a),
        grid=(n // tm,),
        in_specs=_split_specs(tm, tiles_a) + [
            pl.BlockSpec((1, D_MODEL), const),
            pl.BlockSpec((D_MODEL, in_cols), const),
            pl.BlockSpec((1, ATTN_WIDTH), const),
            pl.BlockSpec((1, KV_WIDTH), const),
            pl.BlockSpec((tm, LANES), pos),
            pl.BlockSpec((tm, LANES), pos),
            pl.BlockSpec((1, 2 * D_MODEL), const),
            pl.BlockSpec((ATTN_WIDTH, ATTN_WIDTH), const),
        ],
        out_specs=[
            pl.BlockSpec((tm, ATTN_WIDTH), row),
            pl.BlockSpec((tm, KV_WIDTH), row),
            pl.BlockSpec((KV_WIDTH, tm), lambda i: (i // n_seq_tiles, i % n_seq_tiles)),
            pl.BlockSpec((tm, CONV_WIDTH), row),
            pl.BlockSpec((tm, 2 * D_MODEL), row),
        ],
        out_shape=[
            jax.ShapeDtypeStruct((n, ATTN_WIDTH), BF16),
            jax.ShapeDtypeStruct((n, KV_WIDTH), BF16),
            jax.ShapeDtypeStruct((n // seq * KV_WIDTH, seq), BF16),
            jax.ShapeDtypeStruct((n, CONV_WIDTH), BF16),
            jax.ShapeDtypeStruct((n, 2 * D_MODEL), BF16),
        ],
        compiler_params=_cparams(1, 48),
        name="inproj",
    )(xa, xb, gmix, w_in, gq_t, gk_t, cos_t, sin_t, b_gate, seg)


def _attn_kernel(q_ref, k_ref, vt_ref, o_ref, knorm_ref):
    tq = q_ref.shape[0]
    group = N_HEADS // N_KV_HEADS

    @pl.when(pl.program_id(1) == 0)
    def _():
        for g in range(N_KV_HEADS):
            kf = k_ref[:, g * HEAD_DIM:(g + 1) * HEAD_DIM].astype(F32)
            k_sq = jnp.max(jnp.sum(kf * kf, axis=-1, keepdims=True), axis=0, keepdims=True)
            knorm_ref[g * SUBLANES:(g + 1) * SUBLANES, :] = jnp.broadcast_to(jnp.sqrt(k_sq), (SUBLANES, LANES))

    def run(exact_shift):
        k_norm = [knorm_ref[g * SUBLANES:g * SUBLANES + 1, 0:1] for g in range(N_KV_HEADS)]
        outs, inv_sums, min_sum = [], [], None
        for hd in range(N_HEADS):
            g = hd // group
            qh = q_ref[:, hd * HEAD_DIM:(hd + 1) * HEAD_DIM]
            s = lax.dot_general(qh, k_ref[:, g * HEAD_DIM:(g + 1) * HEAD_DIM], NT_DIMS, preferred_element_type=F32)
            if exact_shift:
                row_shift = jnp.max(s, axis=-1, keepdims=True)
            else:
                qf = qh.astype(F32)
                row_shift = jnp.sqrt(jnp.sum(qf * qf, axis=-1, keepdims=True)) * k_norm[g]
            p = jnp.exp2(s - row_shift)
            l = jnp.sum(p, axis=-1, keepdims=True)
            outs.append(lax.dot_general(vt_ref[g * HEAD_DIM:(g + 1) * HEAD_DIM, :], p.astype(BF16), NT_DIMS,
                                        preferred_element_type=F32))
            inv_sums.append(jnp.broadcast_to(1.0 / l, (tq, HEAD_DIM)))
            min_sum = l if min_sum is None else jnp.minimum(min_sum, l)
        o = jnp.concatenate(outs, axis=0).T * jnp.concatenate(inv_sums, axis=1)
        o_ref[...] = o.astype(BF16)
        return min_sum

    min_sum = run(exact_shift=False)

    @pl.when(jnp.min(min_sum) < ATTN_MIN_ROW_SUM)
    def _():
        run(exact_shift=True)


CONV_HALO = 16
CONV_CHUNK = 64
CONV_WINDOW = CONV_CHUNK + 2 * CONV_HALO
CONV_SPAN = CONV_CHUNK + (CONV_KERNEL // SUBLANES) * SUBLANES


def _conv_shift_matrix():
    m = np.zeros((SUBLANES * CONV_SPAN, CONV_WINDOW), np.float32)
    for res in range(SUBLANES):
        rows = np.arange(CONV_SPAN)
        m[res * CONV_SPAN + rows, rows + res] = 1.0
    return jnp.asarray(m, BF16)


def _conv_chunk(pad_ref, shift_ref, w_ref, b_ref, g_ref, beta_ref, t0):
    shift = CONV_HALO - CONV_KERNEL // 2
    win = pad_ref[pl.ds(t0, CONV_WINDOW), :]
    acc = jnp.zeros((CONV_CHUNK, CONV_WIDTH), F32)
    for res in range(SUBLANES):
        shifted = _dot(shift_ref[res * CONV_SPAN:(res + 1) * CONV_SPAN, :], win)
        for tap in range(CONV_KERNEL):
            if (tap + shift) % SUBLANES == res:
                r0 = (tap + shift) - res
                acc = acc + shifted[r0:r0 + CONV_CHUNK] * w_ref[tap:tap + 1, :]
    acc = acc + b_ref[...]
    mu = jnp.mean(acc, axis=-1, keepdims=True)
    xc = acc - mu
    y = xc * lax.rsqrt(jnp.mean(xc * xc, axis=-1, keepdims=True) + EPS) * g_ref[...] + beta_ref[...]
    return (y * _sigmoid(y)).astype(BF16)


def _attn_conv_kernel(q_ref, k_ref, vt_ref, u_ref, shift_ref, w_ref, b_ref, g_ref, beta_ref,
                      o_ref, c_ref, knorm_ref, pad_ref):
    seq = u_ref.shape[0]
    tq = q_ref.shape[0]

    @pl.when(pl.program_id(1) == 0)
    def _():
        zeros = jnp.zeros((CONV_HALO, CONV_WIDTH), BF16)
        pad_ref[0:CONV_HALO, :] = zeros
        pad_ref[CONV_HALO + seq:CONV_HALO + seq + CONV_HALO, :] = zeros
        pad_ref[CONV_HALO:CONV_HALO + seq, :] = u_ref[...]

    base = pl.multiple_of(pl.program_id(1) * tq, tq)
    for c in range(tq // CONV_CHUNK):
        c_ref[c * CONV_CHUNK:(c + 1) * CONV_CHUNK, :] = _conv_chunk(
            pad_ref, shift_ref, w_ref, b_ref, g_ref, beta_ref, base + c * CONV_CHUNK)
    _attn_kernel(q_ref, k_ref, vt_ref, o_ref, knorm_ref)


def _attention_conv(q, k, vt, u, conv_w, conv_b, ln_g, ln_b, batch, seq, tq):
    n = q.shape[0]
    nq = seq // tq
    const = lambda b, i: (0, 0)
    row = lambda b, i: (b * nq + i, 0)
    per_b = lambda b, i: (b, 0)
    return pl.pallas_call(
        _attn_conv_kernel,
        grid=(batch, nq),
        in_specs=[
            pl.BlockSpec((tq, ATTN_WIDTH), row),
            pl.BlockSpec((seq, KV_WIDTH), per_b),
            pl.BlockSpec((KV_WIDTH, seq), per_b),
            pl.BlockSpec((seq, CONV_WIDTH), per_b),
            pl.BlockSpec((SUBLANES * CONV_SPAN, CONV_WINDOW), const),
            pl.BlockSpec((CONV_KERNEL, CONV_WIDTH), const),
            pl.BlockSpec((1, CONV_WIDTH), const),
            pl.BlockSpec((1, CONV_WIDTH), const),
            pl.BlockSpec((1, CONV_WIDTH), const),
        ],
        out_specs=[pl.BlockSpec((tq, ATTN_WIDTH), row), pl.BlockSpec((tq, CONV_WIDTH), row)],
        out_shape=[jax.ShapeDtypeStruct((n, ATTN_WIDTH), BF16), jax.ShapeDtypeStruct((n, CONV_WIDTH), BF16)],
        scratch_shapes=[
            pltpu.VMEM((N_KV_HEADS * SUBLANES, LANES), F32),
            pltpu.VMEM((seq + 2 * CONV_HALO, CONV_WIDTH), BF16),
        ],
        compiler_params=_cparams(2, 48),
        name="gqa_attention_conv",
    )(q, k, vt, u, _conv_shift_matrix(), conv_w, conv_b, ln_g, ln_b)


def _merge_kernel(xa_ref, xb_ref, a_ref, c_ref, gate_ref, wup_ref, wco_ref, wout_ref, gx_ref, wxq_ref,
                  x1_ref, qx_ref, *, tiles_a):
    attn = _dot(a_ref[...], wup_ref[...])
    conv = _dot(c_ref[...], wco_ref[...])
    merged = gate_ref[:, :D_MODEL].astype(F32) * attn + gate_ref[:, D_MODEL:].astype(F32) * conv
    x1 = _select_rows(xa_ref, xb_ref, tiles_a) + _dot(merged.astype(BF16), wout_ref[...])
    x1_ref[...] = x1
    r = lax.rsqrt(jnp.mean(x1 * x1, axis=-1, keepdims=True) + EPS) * (XHEAD_DIM ** -0.5)
    qx_ref[...] = (_dot((x1 * gx_ref[...]).astype(BF16), wxq_ref[...]) * r).astype(BF16)


def _merge(xa, xb, n, attn_o, conv_o, gates, w_up, w_co, w_out, g_cross, w_xq, tm):
    tiles_a = min(xa.shape[0], n) // tm
    row = lambda i: (i, 0)
    const = lambda i: (0, 0)
    return pl.pallas_call(
        functools.partial(_merge_kernel, tiles_a=tiles_a),
        grid=(n // tm,),
        in_specs=_split_specs(tm, tiles_a) + [
            pl.BlockSpec((tm, ATTN_WIDTH), row),
            pl.BlockSpec((tm, CONV_WIDTH), row),
            pl.BlockSpec((tm, 2 * D_MODEL), row),
            pl.BlockSpec((ATTN_WIDTH, D_MODEL), const),
            pl.BlockSpec((CONV_WIDTH, D_MODEL), const),
            pl.BlockSpec((D_MODEL, D_MODEL), const),
            pl.BlockSpec((1, D_MODEL), const),
            pl.BlockSpec((D_MODEL, D_MODEL), const),
        ],
        out_specs=[pl.BlockSpec((tm, D_MODEL), row), pl.BlockSpec((tm, D_MODEL), row)],
        out_shape=[jax.ShapeDtypeStruct((n, D_MODEL), F32), jax.ShapeDtypeStruct((n, D_MODEL), BF16)],
        compiler_params=_cparams(1, 48),
        name="merge_out_xq",
    )(xa, xb, attn_o, conv_o, gates, w_up, w_co, w_out, g_cross, w_xq)


def _memkv_kernel(m_ref, g_ref, w_ref, k_ref, v_ref):
    h = _rms(m_ref[...], g_ref[...]).astype(BF16)
    k_ref[...] = _dot(h, w_ref[:, :D_MODEL]).astype(BF16)
    v_ref[...] = _dot(h, w_ref[:, D_MODEL:]).astype(BF16)


def _mem_kv(mem, g_mem, w_xkv, tm):
    n = mem.shape[0]
    row = lambda i: (i, 0)
    const = lambda i: (0, 0)
    return pl.pallas_call(
        _memkv_kernel,
        grid=(n // tm,),
        in_specs=[
            pl.BlockSpec((tm, D_MODEL), row),
            pl.BlockSpec((1, D_MODEL), const),
            pl.BlockSpec((D_MODEL, 2 * D_MODEL), const),
        ],
        out_specs=[pl.BlockSpec((tm, D_MODEL), row), pl.BlockSpec((tm, D_MODEL), row)],
        out_shape=[jax.ShapeDtypeStruct((n, D_MODEL), BF16), jax.ShapeDtypeStruct((n, D_MODEL), BF16)],
        compiler_params=_cparams(1, 32),
        name="mem_kv",
    )(mem, g_mem, w_xkv)


def _route(lg, tri, base_ref):
    t = lg.shape[1]
    gl = lg[0:SUBLANES]
    gmax = jnp.max(gl, axis=0, keepdims=True)
    p_top = 1.0 / jnp.sum(jnp.exp(gl - gmax), axis=0, keepdims=True)
    iota_e = lax.broadcasted_iota(I32, (SUBLANES, t), 0).astype(F32)
    none = float(SUBLANES)
    grp = jnp.min(jnp.where(gl == gmax, iota_e, none), axis=0, keepdims=True)
    el = lg[SUBLANES:SUBLANES + N_EXPERTS]
    sel = el[0:EXPERTS_PER_GROUP]
    for g in range(1, N_GROUPS):
        sel = jnp.where(grp == float(g), el[g * EXPERTS_PER_GROUP:(g + 1) * EXPERTS_PER_GROUP], sel)
    v0 = jnp.max(sel, axis=0, keepdims=True)
    i0 = jnp.min(jnp.where(sel == v0, iota_e, none), axis=0, keepdims=True)
    rest = jnp.where(iota_e == i0, NEG_BIG, sel)
    v1 = jnp.max(rest, axis=0, keepdims=True)
    i1 = jnp.min(jnp.where(rest == v1, iota_e, none), axis=0, keepdims=True)
    ratio = jnp.exp(v1 - v0)
    g0 = p_top / (1.0 + ratio)
    g1 = p_top * ratio / (1.0 + ratio)
    e0 = grp * float(EXPERTS_PER_GROUP) + i0
    e1 = grp * float(EXPERTS_PER_GROUP) + i1
    iota_all = lax.broadcasted_iota(I32, (N_EXPERTS, t), 0).astype(F32)
    hit0 = iota_all == e0
    hit1 = iota_all == e1
    onehot = jnp.where(hit0 | hit1, 1.0, 0.0)
    before = _dot(onehot.astype(BF16), tri) + base_ref[:, 0:1]
    r0 = jnp.sum(jnp.where(hit0, before, 0.0), axis=0, keepdims=True)
    r1 = jnp.sum(jnp.where(hit1, before, 0.0), axis=0, keepdims=True)
    base_ref[...] = base_ref[...] + jnp.sum(onehot, axis=1, keepdims=True)
    return e0.astype(I32), e1.astype(I32), r0.astype(I32), r1.astype(I32), g0, g1


def _cross_kernel(x_ref, q_ref, k_ref, v_ref, wo_ref, gffn_ref, wrh_ref, wrl_ref, rb_ref, tri_ref,
                  x2_ref, hp_ref, idx_ref, gate_ref, cnt_ref, base_ref):
    @pl.when((pl.program_id(0) == 0) & (pl.program_id(1) == 0))
    def _():
        base_ref[...] = jnp.zeros_like(base_ref)

    t = tri_ref.shape[0]
    for sub in range(x_ref.shape[0] // t):
        rows = slice(sub * t, (sub + 1) * t)
        heads = []
        for hd in range(N_XHEADS):
            sl = slice(hd * XHEAD_DIM, (hd + 1) * XHEAD_DIM)
            s = lax.dot_general(q_ref[rows, sl], k_ref[:, sl], NT_DIMS, preferred_element_type=F32)
            m = jnp.max(s, axis=-1, keepdims=True)
            p = jnp.exp(s - m)
            l = jnp.sum(p, axis=-1, keepdims=True)
            heads.append(_dot(p.astype(BF16), v_ref[:, sl]) / l)
        o = jnp.concatenate(heads, axis=1).astype(BF16)
        x2 = x_ref[rows, :] + _dot(o, wo_ref[...])
        x2_ref[rows, :] = x2
        h = _rms(x2, gffn_ref[...])
        h_hi = h.astype(BF16)
        _store_token_rows(hp_ref.at[pl.ds(sub * t * ROW_TILES, t * ROW_TILES)], _pack_bf16_pairs(h))
        h_lo = (h - h_hi.astype(F32)).astype(BF16)
        lg = (lax.dot_general(wrh_ref[...], h_hi, NT_DIMS, preferred_element_type=F32)
              + lax.dot_general(wrh_ref[...], h_lo, NT_DIMS, preferred_element_type=F32)
              + lax.dot_general(wrl_ref[...], h_hi, NT_DIMS, preferred_element_type=F32)) + rb_ref[:, 0:1]
        e0, e1, r0, r1, g0, g1 = _route(lg, tri_ref[...], base_ref)
        idx_ref[:, rows] = jnp.concatenate([e0, e1, r0, r1, jnp.zeros((SUBLANES - 4, t), I32)], axis=0)
        gl = jnp.concatenate([g0, g1, jnp.zeros((LANES - 2, t), F32)], axis=0)
        gate_ref[rows, :] = gl.T
    cnt_ref[...] = base_ref[...]


def _cross_route(x1, qx, kx, vx, w_xo, g_ffn, wr_hi, wr_lo, r_bias, tri, batch, seq, mem_len, tq):
    n = x1.shape[0]
    nq = seq // tq
    row = lambda b, i: (b * nq + i, 0)
    per_b = lambda b, i: (b, 0)
    const = lambda b, i: (0, 0)
    return pl.pallas_call(
        _cross_kernel,
        grid=(batch, nq),
        in_specs=[
            pl.BlockSpec((tq, D_MODEL), row),
            pl.BlockSpec((tq, D_MODEL), row),
            pl.BlockSpec((mem_len, D_MODEL), per_b),
            pl.BlockSpec((mem_len, D_MODEL), per_b),
            pl.BlockSpec((D_MODEL, D_MODEL), const),
            pl.BlockSpec((1, D_MODEL), const),
            pl.BlockSpec((ROUTER_ROWS, D_MODEL), const),
            pl.BlockSpec((ROUTER_ROWS, D_MODEL), const),
            pl.BlockSpec((ROUTER_ROWS, LANES), const),
            pl.BlockSpec(tri.shape, const),
        ],
        out_specs=[
            pl.BlockSpec((tq, D_MODEL), row),
            pl.BlockSpec((tq * ROW_TILES, LANES), row),
            pl.BlockSpec((SUBLANES, tq), lambda b, i: (0, b * nq + i)),
            pl.BlockSpec((tq, LANES), row),
            pl.BlockSpec((N_EXPERTS, LANES), const),
        ],
        out_shape=[
            jax.ShapeDtypeStruct((n, D_MODEL), F32),
            jax.ShapeDtypeStruct((n * ROW_TILES, LANES), U32),
            jax.ShapeDtypeStruct((SUBLANES, n), I32),
            jax.ShapeDtypeStruct((n, LANES), F32),
            jax.ShapeDtypeStruct((N_EXPERTS, LANES), F32),
        ],
        scratch_shapes=[pltpu.VMEM((N_EXPERTS, LANES), F32)],
        compiler_params=_cparams(2, 48),
        name="cross_attn_router",
    )(x1, qx, kx, vx, w_xo, g_ffn, wr_hi, wr_lo, r_bias, tri)


DISPATCH_TOKENS = 512
ISSUE_UNROLL = 8


def _row_copy(src_ref, src_tok, dst_ref, dst_tok, sem):
    return pltpu.make_async_copy(src_ref.at[pl.ds(src_tok * ROW_TILES, ROW_TILES)],
                                 dst_ref.at[pl.ds(dst_tok * ROW_TILES, ROW_TILES)], sem)


def _slots_copy(slots_hbm, tile, slots_smem, sem):
    return pltpu.make_async_copy(slots_hbm.at[tile], slots_smem.at[tile % 2], sem)


def _dispatch_kernel(zero_blocks, slots_hbm, hp_ref, xs_out, slots_smem, zero_ref, idx_sem, row_sem, zero_sem):
    block_rows = EXPERT_ROWS * ROW_TILES

    def zero_copy(j):
        return pltpu.make_async_copy(zero_ref, xs_out.at[pl.ds(zero_blocks[j] * block_rows, block_rows)], zero_sem)

    step = pl.program_id(0)

    @pl.when(step == 0)
    def _():
        _slots_copy(slots_hbm, step, slots_smem, idx_sem).start()
        zero_ref[...] = jnp.zeros_like(zero_ref)
        for j in range(zero_blocks.shape[0]):
            zero_copy(j).start()
        for j in range(zero_blocks.shape[0]):
            zero_copy(j).wait()

    _slots_copy(slots_hbm, step, slots_smem, idx_sem).wait()

    @pl.when(step + 1 < pl.num_programs(0))
    def _():
        _slots_copy(slots_hbm, step + 1, slots_smem, idx_sem).start()

    buf = step % 2

    def issue(group, c, carry):
        for k in range(2):
            slot = slots_smem[buf, k * (SUBLANES // 2) + group, c]
            _row_copy(hp_ref, group * LANES + c, xs_out, slot, row_sem).start(priority=k)
        return carry

    for group in range(DISPATCH_TOKENS // LANES):
        lax.fori_loop(0, LANES, functools.partial(issue, group), 0, unroll=ISSUE_UNROLL)

    def drain(r, carry):
        _row_copy(hp_ref, 0, xs_out, 0, row_sem).wait()
        return carry

    lax.fori_loop(0, 2 * DISPATCH_TOKENS, drain, 0, unroll=64)


def _dispatch(zero_blocks, slots, hp, n_blocks):
    n = hp.shape[0] // ROW_TILES
    block_rows = EXPERT_ROWS * ROW_TILES
    grid_spec = pltpu.PrefetchScalarGridSpec(
        num_scalar_prefetch=1,
        grid=(n // DISPATCH_TOKENS,),
        in_specs=[
            pl.BlockSpec(memory_space=pl.ANY),
            pl.BlockSpec((DISPATCH_TOKENS * ROW_TILES, LANES), lambda i, zb: (i, 0)),
        ],
        out_specs=pl.BlockSpec(memory_space=pl.ANY),
        scratch_shapes=[
            pltpu.SMEM((2, SUBLANES, LANES), I32),
            pltpu.VMEM((block_rows, LANES), U32),
            pltpu.SemaphoreType.DMA,
            pltpu.SemaphoreType.DMA,
            pltpu.SemaphoreType.DMA,
        ],
    )
    return pl.pallas_call(
        _dispatch_kernel,
        grid_spec=grid_spec,
        out_shape=jax.ShapeDtypeStruct((n_blocks * block_rows, LANES), U32),
        compiler_params=_cparams(1, 16),
        name="moe_dispatch",
    )(zero_blocks, slots, hp)


def _expert_kernel(be_ref, nused_ref, xs_ref, wg_ref, wu_ref, wd_ref, ys_ref, wgb_ref, wub_ref, wdb_ref):
    step = pl.program_id(0)
    live = step < nused_ref[0]

    @pl.when((step == 0) | (be_ref[step] != be_ref[jnp.maximum(step - 1, 0)]))
    def _():
        wgb_ref[...] = wg_ref[...].astype(BF16)
        wub_ref[...] = wu_ref[...].astype(BF16)
        wdb_ref[...] = wd_ref[...].astype(BF16)

    @pl.when(live)
    def _():
        x = _unpack_bf16_pairs(_load_token_rows(xs_ref)).astype(BF16)
        gate = _dot(x, wgb_ref[...])
        hid = (gate * _sigmoid(gate)) * _dot(x, wub_ref[...])
        _store_token_rows(ys_ref, _pack_bf16_pairs(_dot(hid.astype(BF16), wdb_ref[...])))

    @pl.when(jnp.logical_not(live))
    def _():
        ys_ref[...] = jnp.zeros_like(ys_ref)


def _experts(block_e, n_used, xs, w_gate, w_up, w_down, layer):
    n_blocks = xs.shape[0] // (EXPERT_ROWS * ROW_TILES)
    rows = lambda i, be, nu: (i, 0)
    by_expert = lambda i, be, nu: (layer, be[i], 0, 0)
    grid_spec = pltpu.PrefetchScalarGridSpec(
        num_scalar_prefetch=2,
        grid=(n_blocks,),
        in_specs=[
            pl.BlockSpec((EXPERT_ROWS * ROW_TILES, LANES), rows),
            pl.BlockSpec((None, None, D_MODEL, D_FF_EXPERT), by_expert),
            pl.BlockSpec((None, None, D_MODEL, D_FF_EXPERT), by_expert),
            pl.BlockSpec((None, None, D_FF_EXPERT, D_MODEL), by_expert),
        ],
        out_specs=pl.BlockSpec((EXPERT_ROWS * ROW_TILES, LANES), rows),
        scratch_shapes=[
            pltpu.VMEM((D_MODEL, D_FF_EXPERT), BF16),
            pltpu.VMEM((D_MODEL, D_FF_EXPERT), BF16),
            pltpu.VMEM((D_FF_EXPERT, D_MODEL), BF16),
        ],
    )
    return pl.pallas_call(
        _expert_kernel,
        grid_spec=grid_spec,
        out_shape=jax.ShapeDtypeStruct(xs.shape, U32),
        compiler_params=_cparams(1, 40),
        name="moe_experts",
    )(block_e, n_used, xs, w_gate, w_up, w_down)


def _combine_kernel(slots_hbm, ys_hbm, x_ref, gate_ref, gfin_ref, *out_and_scratch, tiles_a):
    if tiles_a is None:
        x3_ref, rows_ref, slots_smem, idx_sem, row_sems = out_and_scratch
    else:
        ya_ref, yb_ref, rows_ref, slots_smem, idx_sem, row_sems = out_and_scratch
    step = pl.program_id(0)

    n_steps = pl.num_programs(0)

    def issue_rows(tile):
        buf = tile % 2

        def issue(group, c, carry):
            for k in range(2):
                slot = slots_smem[buf, k * (SUBLANES // 2) + group, c]
                _row_copy(ys_hbm, slot, rows_ref.at[buf, k], group * LANES + c, row_sems.at[buf]).start(priority=k)
            return carry

        for group in range(DISPATCH_TOKENS // LANES):
            lax.fori_loop(0, LANES, functools.partial(issue, group), 0, unroll=ISSUE_UNROLL)

    @pl.when(step == 0)
    def _():
        _slots_copy(slots_hbm, step, slots_smem, idx_sem).start()
        _slots_copy(slots_hbm, step, slots_smem, idx_sem).wait()

        @pl.when(n_steps > 1)
        def _():
            _slots_copy(slots_hbm, step + 1, slots_smem, idx_sem).start()

        issue_rows(step)

    @pl.when(step + 1 < n_steps)
    def _():
        _slots_copy(slots_hbm, step + 1, slots_smem, idx_sem).wait()

        @pl.when(step + 2 < n_steps)
        def _():
            _slots_copy(slots_hbm, step + 2, slots_smem, idx_sem).start()

        issue_rows(step + 1)

    cur = step % 2

    def drain(r, carry):
        _row_copy(ys_hbm, 0, rows_ref.at[cur, 0], 0, row_sems.at[cur]).wait()
        return carry

    lax.fori_loop(0, 2 * DISPATCH_TOKENS, drain, 0, unroll=64)

    y0 = _unpack_bf16_pairs(_load_token_rows(rows_ref.at[cur, 0]))
    y1 = _unpack_bf16_pairs(_load_token_rows(rows_ref.at[cur, 1]))
    x3 = x_ref[...] + gate_ref[:, 0:1] * y0 + gate_ref[:, 1:2] * y1
    if tiles_a is None:
        x3_ref[...] = x3
    else:
        y = _rms(x3, gfin_ref[...])

        @pl.when(step < tiles_a)
        def _():
            ya_ref[...] = y

        @pl.when(step >= tiles_a)
        def _():
            yb_ref[...] = y


def _combine(slots, ys, x2, gates, g_final, rows_a):
    n = x2.shape[0]
    row = lambda i: (i, 0)
    const = lambda i: (0, 0)
    tok = DISPATCH_TOKENS
    if rows_a is None:
        tiles_a = None
        out_specs = [pl.BlockSpec((tok, D_MODEL), row)]
        out_shape = [jax.ShapeDtypeStruct((n, D_MODEL), F32)]
    else:
        tiles_a = rows_a // tok
        out_specs = [pl.BlockSpec((tok, D_MODEL), lambda i: (jnp.minimum(i, tiles_a - 1), 0)),
                     pl.BlockSpec((tok, D_MODEL), lambda i: (jnp.maximum(i - tiles_a, 0), 0))]
        out_shape = [jax.ShapeDtypeStruct((rows_a, D_MODEL), F32), jax.ShapeDtypeStruct((n - rows_a, D_MODEL), F32)]
    out = pl.pallas_call(
        functools.partial(_combine_kernel, tiles_a=tiles_a),
        grid=(n // tok,),
        in_specs=[
            pl.BlockSpec(memory_space=pl.ANY),
            pl.BlockSpec(memory_space=pl.ANY),
            pl.BlockSpec((tok, D_MODEL), row),
            pl.BlockSpec((tok, LANES), row),
            pl.BlockSpec((1, D_MODEL), const),
        ],
        out_specs=out_specs,
        out_shape=out_shape,
        scratch_shapes=[
            pltpu.VMEM((2, 2, tok * ROW_TILES, LANES), U32),
            pltpu.SMEM((2, SUBLANES, LANES), I32),
            pltpu.SemaphoreType.DMA,
            pltpu.SemaphoreType.DMA((2,)),
        ],
        compiler_params=_cparams(1, 32),
        name="moe_combine" if rows_a is None else "moe_combine_final",
    )(slots, ys, x2, gates, g_final)
    return out


def _rope_tables(seq):
    rows = seq // GRID_W
    row_idx = jnp.repeat(jnp.arange(rows, dtype=F32), GRID_W)
    col_idx = jnp.tile(jnp.arange(GRID_W, dtype=F32), rows)
    n_freq = HEAD_DIM // 4
    inv_freq = ROPE_THETA ** (-jnp.arange(n_freq, dtype=F32) / n_freq)
    ang_row = row_idx[:, None] * inv_freq
    ang_col = col_idx[:, None] * inv_freq
    cos = jnp.concatenate([jnp.cos(ang_row)] * 2 + [jnp.cos(ang_col)] * 2, axis=1)
    sin = jnp.concatenate([-jnp.sin(ang_row), jnp.sin(ang_row), -jnp.sin(ang_col), jnp.sin(ang_col)], axis=1)
    return jnp.tile(cos, (1, LANES // HEAD_DIM)), jnp.tile(sin, (1, LANES // HEAD_DIM))


def _segment_ones():
    head = np.arange(ATTN_WIDTH) // HEAD_DIM
    return jnp.asarray(head[:, None] == head[None, :], BF16)


def _router_weights(w_group, b_group, w_router, b_router):
    w = jnp.zeros((ROUTER_ROWS, D_MODEL), F32)
    w = w.at[0:N_GROUPS].set(w_group.T).at[SUBLANES:SUBLANES + N_EXPERTS].set(w_router.T)
    hi = w.astype(BF16)
    lo = (w - hi.astype(F32)).astype(BF16)
    b = jnp.full((ROUTER_ROWS,), NEG_BIG, F32).at[0:N_GROUPS].set(b_group)
    b = b.at[SUBLANES:SUBLANES + N_EXPERTS].set(b_router)
    return hi, lo, jnp.tile(b[:, None], (1, LANES))


def _slot_plan(idx, counts, n):
    cnt = counts[:, 0].astype(I32)
    padded = (cnt + EXPERT_ROWS - 1) // EXPERT_ROWS * EXPERT_ROWS
    pad_end = jnp.cumsum(padded)
    pad_start = pad_end - padded
    experts = jnp.arange(N_EXPERTS, dtype=I32)
    start_of = lambda e: jnp.sum(jnp.where(e[:, None] == experts[None, :], pad_start[None, :], 0), axis=1)
    slot0 = start_of(idx[0]) + idx[2]
    slot1 = start_of(idx[1]) + idx[3]
    n_tiles = n // DISPATCH_TOKENS
    per_k = SUBLANES // 2
    slots = jnp.stack([slot0, slot1]).reshape(2, n_tiles, per_k, LANES)
    slots = slots.transpose(1, 0, 2, 3).reshape(n_tiles, SUBLANES, LANES)
    n_blocks = (2 * n) // EXPERT_ROWS + N_EXPERTS
    block_first_row = jnp.arange(n_blocks, dtype=I32) * EXPERT_ROWS
    ends_passed = jnp.sum((pad_end[None, :] <= block_first_row[:, None]).astype(I32), axis=1)
    block_e = jnp.minimum(ends_passed, N_EXPERTS - 1)
    n_used = (pad_end[-1:] // EXPERT_ROWS).astype(I32)
    last_block = jnp.where(padded > 0, pad_end // EXPERT_ROWS - 1, n_blocks - 1)
    unused = jnp.minimum(n_used[0] + jnp.arange(N_EXPERTS, dtype=I32), n_blocks - 1)
    need = jnp.zeros((n_blocks,), I32).at[jnp.concatenate([last_block, unused])].set(1)
    zero_blocks = jnp.argsort(1 - need, stable=True)[:min(2 * N_EXPERTS, n_blocks)].astype(I32)
    return slots, block_e, n_used, n_blocks, zero_blocks


TOKEN_TILE = 512


def _plan_tiles(seq, batch, mem_len):
    tm = min(TOKEN_TILE, seq)
    assert seq % tm == 0 and seq % CONV_CHUNK == 0 and seq % GRID_W == 0
    return tm, tm, mem_len * (2 if batch % 2 == 0 else 1)


def kernel(x_prompt, x_sample, mem_prompt, mem_sample, g_mix, w_in, g_q, g_k, b_gate, w_attn_up, conv_w, conv_b, ln_conv_g, ln_conv_b, w_conv_out, w_out, g_cross, g_mem, w_xq, w_xkv, w_xo, g_ffn, w_group, b_group, w_router, b_router, w_e_gate, w_e_up, w_e_down, g_final):
    bp, seq, d = x_prompt.shape
    bs = x_sample.shape[0]
    assert x_sample.shape[1] == seq and d == D_MODEL and seq % GRID_W == 0
    batch = bp + bs
    mem_len = mem_prompt.shape[1]
    n = batch * seq
    depth = g_mix.shape[0]
    tm, tq_attn, mem_tile = _plan_tiles(seq, batch, mem_len)
    assert n % DISPATCH_TOKENS == 0 and (bp * seq) % DISPATCH_TOKENS == 0 and (2 * n) % EXPERT_ROWS == 0

    xa = x_prompt.reshape(bp * seq, d)
    xb = x_sample.reshape(bs * seq, d)
    mem = jnp.concatenate([mem_prompt.reshape(bp * mem_len, d), mem_sample.reshape(bs * mem_len, d)], axis=0)
    cos_t, sin_t = _rope_tables(seq)
    seg = _segment_ones()
    tri = jnp.asarray(np.triu(np.ones((tm, tm), np.float32), k=1), BF16)
    row2 = lambda a: a.reshape(1, -1).astype(F32)

    out = None
    for l in range(depth):
        q, k, v, u, gates = _inproj(
            xa, xb, n, row2(g_mix[l]), w_in[l].astype(BF16), row2(jnp.tile(g_q[l], N_HEADS)),
            row2(jnp.tile(g_k[l], N_KV_HEADS)), cos_t, sin_t, row2(b_gate[l]), seg, seq, tm)
        attn_o, conv_o = _attention_conv(q, k, v, u, conv_w[l], row2(conv_b[l]), row2(ln_conv_g[l]),
                                         row2(ln_conv_b[l]), batch, seq, tq_attn)
        x1, qx = _merge(xa, xb, n, attn_o, conv_o, gates, w_attn_up[l].astype(BF16), w_conv_out[l].astype(BF16),
                        w_out[l].astype(BF16), row2(g_cross[l]), w_xq[l].astype(BF16), tm)
        kx, vx = _mem_kv(mem, row2(g_mem[l]), w_xkv[l].astype(BF16), mem_tile)
        wr_hi, wr_lo, r_bias = _router_weights(w_group[l], b_group[l], w_router[l], b_router[l])
        x2, hp, idx, gate_tm, counts = _cross_route(
            x1, qx, kx, vx, w_xo[l].astype(BF16), row2(g_ffn[l]), wr_hi, wr_lo, r_bias, tri,
            batch, seq, mem_len, tm)
        slots, block_e, n_used, n_blocks, zero_blocks = _slot_plan(idx, counts, n)
        xs = _dispatch(zero_blocks, slots, hp, n_blocks)
        ys = _experts(block_e, n_used, xs, w_e_gate, w_e_up, w_e_down, l)
        last = l == depth - 1
        out = _combine(slots, ys, x2, gate_tm, row2(g_final), bp * seq if last else None)
        xa = xb = out[0]
    return (out[0].reshape(bp, seq, d), out[1].reshape(bs, seq, d))
```
